```python
import jax
import jax.numpy as jnp
from jax import lax
import numpy as np

D_MODEL = 4096
BATCH = 1
SEQ = 8192
DEPTH = 4
DEC_BATCH = 16
DEC_SEQ = 64
PAST_LEN = 2048

CHUNK = 64
QBLOCK = 128
EPS = 1e-6
NEG_BIG = -1e30
ROPE_THETA = 10000.0
HEAD_DIM = 128
D_FF = ((8 * D_MODEL // 3 + 127) // 128) * 128
D_PLE = 256
N_EVEN = (DEPTH + 1) // 2
N_ODD = DEPTH // 2
A_HEADS = D_MODEL // 256
A_KV_HEADS = A_HEADS // 4
IDX_HEADS = D_MODEL // 128
IDX_DIM = 128
IDX_ROPE_DIM = 64
TOPK_MAX = 256
IDX_SCALE = (IDX_HEADS * IDX_DIM) ** -0.5
B_HEADS = D_MODEL // 256
B_DK = 128
B_DV = 128
C_QK_HEADS = D_MODEL // 256
C_V_HEADS = D_MODEL // 128
C_DK = 128
C_DV = 128
C_CONV = 4
C_CONV_CH = 2 * C_QK_HEADS * C_DK + C_V_HEADS * C_DV
EVEN_COLS = (A_HEADS * HEAD_DIM, A_KV_HEADS * HEAD_DIM, A_KV_HEADS * HEAD_DIM,
             IDX_HEADS * IDX_DIM, IDX_DIM, IDX_HEADS,
             B_HEADS * B_DK, B_HEADS * B_DK, B_HEADS * B_DV, B_HEADS * B_DV)
EVEN_IN = sum(EVEN_COLS)
EVEN_OUT = A_HEADS * HEAD_DIM + B_HEADS * B_DV
ODD_COLS = (C_CONV_CH, C_V_HEADS * C_DV, C_V_HEADS, C_V_HEADS)
ODD_IN = sum(ODD_COLS)
ODD_OUT = C_V_HEADS * C_DV

kernel_name = 'hybrid_streaming_dsa_hgrn2_gdn_step'


def split_cols(x, sizes):
    out, start = [], 0
    for s in sizes:
        out.append(x[..., start:start + s])
        start += s
    return out


def rms_norm(x, w):
    xf = x.astype(jnp.float32)
    y = xf * lax.rsqrt(jnp.mean(xf * xf, axis=-1, keepdims=True) + EPS)
    return (y * w.astype(jnp.float32)).astype(x.dtype)


def l2norm(x):
    return x * lax.rsqrt(jnp.sum(x * x, axis=-1, keepdims=True) + EPS)


def swiglu(x, wg, wu, wd):
    return (jax.nn.silu(x @ wg) * (x @ wu)) @ wd


def rope(x, pos, rot_dim):
    half = rot_dim // 2
    inv_freq = ROPE_THETA ** (-jnp.arange(half, dtype=jnp.float32) / half)
    ang = pos.astype(jnp.float32)[:, None] * inv_freq[None, :]
    cos = jnp.cos(ang)[None, :, None, :]
    sin = jnp.sin(ang)[None, :, None, :]
    xf = x.astype(jnp.float32)
    x1, x2, rest = xf[..., :half], xf[..., half:rot_dim], xf[..., rot_dim:]
    out = jnp.concatenate([x1 * cos - x2 * sin, x2 * cos + x1 * sin, rest], axis=-1)
    return out.astype(x.dtype)


def dsa_attention(q, k, v, qi, ki, wi, q_pos, k_pos):
    b, t = q.shape[:2]
    l = k.shape[1]
    topk = min(TOPK_MAX, l // 4)
    qb = min(QBLOCK, t)
    nb = t // qb
    grp = A_HEADS // A_KV_HEADS
    k_chunk = k_pos // CHUNK
    scale = HEAD_DIM ** -0.5

    def blocks(a):
        return jnp.moveaxis(a.reshape((b, nb, qb) + a.shape[2:]), 1, 0)

    def one_block(args):
        q_b, qi_b, wi_b, pos_b = args
        q_chunk = pos_b // CHUNK
        logits = jnp.einsum('bqhd,bkd->bqhk', qi_b, ki).astype(jnp.float32)
        score = jnp.einsum('bqh,bqhk->bqk', wi_b.astype(jnp.float32), jax.nn.relu(logits)) * IDX_SCALE
        admissible = k_chunk[None, :] <= q_chunk[:, None]
        score = jnp.where(admissible[None], score, NEG_BIG)
        _, top_idx = lax.top_k(score, topk)
        valid = k_chunk[top_idx] <= q_chunk[None, :, None]
        kg = jax.vmap(lambda kk, ii: kk[ii])(k, top_idx)
        vg = jax.vmap(lambda vv, ii: vv[ii])(v, top_idx)
        qg = q_b.reshape(b, qb, A_KV_HEADS, grp, HEAD_DIM)
        s = jnp.einsum('bqngd,bqknd->bqngk', qg, kg).astype(jnp.float32) * scale
        s = jnp.where(valid[:, :, None, None, :], s, NEG_BIG)
        p = jax.nn.softmax(s, axis=-1)
        o = jnp.einsum('bqngk,bqknd->bqngd', p.astype(v.dtype), vg)
        return o.reshape(b, qb, A_HEADS, HEAD_DIM)

    out = lax.map(one_block, (blocks(q), blocks(qi), blocks(wi), q_pos.reshape(nb, qb)))
    return jnp.moveaxis(out, 0, 1).reshape(b, t, A_HEADS, HEAD_DIM)


def gla_chunked(q, k, g, v, s0):
    b, t, h, dk = q.shape
    dv = v.shape[-1]
    c = min(CHUNK, t)
    n = t // c
    causal = jnp.tril(jnp.ones((c, c), dtype=bool))[None, :, :, None, None]

    def cm(a):
        return jnp.moveaxis(a.reshape((b, n, c) + a.shape[2:]), 1, 0)

    def step(s, xs):
        qc, kc, gc, vc = xs
        gcum = jnp.cumsum(gc, axis=1)
        o_inter = jnp.einsum('bthc,bhcv->bthv', qc * jnp.exp(gcum), s)
        diff = gcum[:, :, None] - gcum[:, None, :]
        w = jnp.where(causal, jnp.exp(jnp.where(causal, diff, 0.0)), 0.0)
        att = jnp.einsum('bthc,bshc,btshc->bths', qc, kc, w)
        o_intra = jnp.einsum('bths,bshv->bthv', att, vc)
        g_last = gcum[:, -1]
        s_new = jnp.exp(g_last)[..., None] * s + jnp.einsum(
            'bshc,bshv->bhcv', kc * jnp.exp(g_last[:, None] - gcum), vc)
        return s_new, o_inter + o_intra

    s_fin, o = lax.scan(step, s0, (cm(q), cm(k), cm(g), cm(v)))
    return jnp.moveaxis(o, 0, 1).reshape(b, t, h, dv), s_fin


def gated_delta_chunked(q, k, v, g, beta, s0):
    b, t, h, dk = q.shape
    dv = v.shape[-1]
    c = min(CHUNK, t)
    n = t // c

    def heads_first(a):
        a = a.reshape((b, n, c, h) + a.shape[3:])
        return jnp.moveaxis(a, 3, 2)

    qh, kh, vh, gh, bh = [heads_first(a) for a in (q, k, v, g, beta)]
    gcum = jnp.cumsum(gh, axis=-1)
    diff = gcum[..., :, None] - gcum[..., None, :]
    incl = jnp.tril(jnp.ones((c, c), dtype=bool))
    strict = jnp.tril(jnp.ones((c, c), dtype=bool), -1)
    decay = jnp.where(incl, jnp.exp(jnp.where(incl, diff, 0.0)), 0.0)
    kb = kh * bh[..., None]
    a_mat = jnp.where(strict, jnp.einsum('bnhtd,bnhsd->bnhts', kb, kh) * decay, 0.0)
    rhs = jnp.concatenate([vh * bh[..., None], kb * jnp.exp(gcum)[..., None]], axis=-1)
    sol = lax.linalg.triangular_solve(a_mat + jnp.eye(c, dtype=a_mat.dtype), rhs,
                                      left_side=True, lower=True, unit_diagonal=True)
    u0, w = sol[..., :dv], sol[..., dv:]
    qk = jnp.einsum('bnhtd,bnhsd->bnhts', qh, kh) * decay
    q_dec = qh * jnp.exp(gcum)[..., None]
    g_last = gcum[..., -1]
    k_dec = kh * jnp.exp(g_last[..., None] - gcum)[..., None]

    def step(s, xs):
        u0c, wc, qkc, qdc, kdc, glc = xs
        u = u0c - jnp.einsum('bhtk,bhkv->bhtv', wc, s)
        o = jnp.einsum('bhtk,bhkv->bhtv', qdc, s) + jnp.einsum('bhts,bhsv->bhtv', qkc, u)
        s = jnp.exp(glc)[..., None, None] * s + jnp.einsum('bhsk,bhsv->bhkv', kdc, u)
        return s, o

    xs = tuple(jnp.moveaxis(a, 1, 0) for a in (u0, w, qk, q_dec, k_dec, g_last))
    s_fin, o = lax.scan(step, s0, xs)
    o = jnp.transpose(o, (1, 0, 3, 2, 4)).reshape(b, t, h, dv)
    return o, s_fin


def hgrn2_mixer(q_raw, f_raw, i_raw, g_raw, lb, norm_w, s0):
    b, t, _ = q_raw.shape
    f32 = jnp.float32
    fr = f_raw.astype(f32)
    lbf = lb.astype(f32)
    k = (1.0 - lbf) * jax.nn.sigmoid(-fr)
    log_f = jnp.log1p(-jnp.minimum(k, 1.0 - 1e-6))
    q = jax.nn.silu(q_raw.astype(f32))
    shp = (b, t, B_HEADS, B_DK)
    o, s_fin = gla_chunked(q.reshape(shp), k.reshape(shp), log_f.reshape(shp),
                           i_raw.astype(f32).reshape(b, t, B_HEADS, B_DV), s0)
    o = rms_norm(o, norm_w) * jax.nn.silu(g_raw.astype(f32).reshape(b, t, B_HEADS, B_DV))
    return o.reshape(b, t, B_HEADS * B_DV).astype(q_raw.dtype), s_fin


def setup_inputs(seed: int = 0) -> dict:
    key = jax.random.key(seed)
    ks = iter(jax.random.split(key, 64))
    f32 = jnp.float32

    def nrm(shape, scale):
        return jax.random.normal(next(ks), shape, f32) * scale

    def gain(shape):
        return 1.0 + 0.1 * jax.random.normal(next(ks), shape, f32)

    d = D_MODEL
    a_log = jnp.log(jax.random.uniform(next(ks), (N_ODD, C_V_HEADS), f32, 1.0, 16.0))
    dt = jnp.exp(jax.random.uniform(next(ks), (N_ODD, C_V_HEADS), f32, np.log(1e-3), np.log(1e-1)))
    dt_bias = dt + jnp.log(-jnp.expm1(-dt))
    return {
        'x_prompt': nrm((BATCH, SEQ, d), 1.0),
        'x_sample': nrm((DEC_BATCH, DEC_SEQ, d), 1.0),
        'cache_a_k': nrm((N_EVEN, DEC_BATCH, PAST_LEN, A_KV_HEADS, HEAD_DIM), 1.0),
        'cache_a_v': nrm((N_EVEN, DEC_BATCH, PAST_LEN, A_KV_HEADS, HEAD_DIM), 1.0),
        'cache_a_kidx': nrm((N_EVEN, DEC_BATCH, PAST_LEN, IDX_DIM), 1.0),
        'state_b': nrm((N_EVEN, DEC_BATCH, B_HEADS, B_DK, B_DV), 0.5),
        'state_c': nrm((N_ODD, DEC_BATCH, C_V_HEADS, C_DK, C_DV), 0.5),
        'state_c_conv': nrm((N_ODD, DEC_BATCH, C_CONV - 1, C_CONV_CH), 1.0),
        'p_prompt': nrm((DEPTH, BATCH, SEQ, D_PLE), 1.0),
        'p_sample': nrm((DEPTH, DEC_BATCH, DEC_SEQ, D_PLE), 1.0),
        'ffn1_norm': gain((DEPTH, d)),
        'ffn1_w_gate': nrm((DEPTH, d, D_FF), d ** -0.5),
        'ffn1_w_up': nrm((DEPTH, d, D_FF), d ** -0.5),
        'ffn1_w_down': nrm((DEPTH, D_FF, d), D_FF ** -0.5),
        'mix_norm': gain((DEPTH, d)),
        'even_w_in': nrm((N_EVEN, d, EVEN_IN), d ** -0.5),
        'even_w_out': nrm((N_EVEN, EVEN_OUT, d), EVEN_OUT ** -0.5),
        'a_q_norm': gain((N_EVEN, HEAD_DIM)),
        'a_k_norm': gain((N_EVEN, HEAD_DIM)),
        'b_lb_logits': nrm((N_EVEN, B_HEADS * B_DK), 0.5),
        'b_out_norm': gain((N_EVEN, B_DV)),
        'odd_w_in': nrm((N_ODD, d, ODD_IN), d ** -0.5),
        'odd_w_out': nrm((N_ODD, ODD_OUT, d), ODD_OUT ** -0.5),
        'c_conv_w': nrm((N_ODD, C_CONV, C_CONV_CH), C_CONV ** -0.5),
        'c_a_log': a_log,
        'c_dt_bias': dt_bias,
        'c_out_norm': gain((N_ODD, C_DV)),
        'ffn2_norm': gain((DEPTH, d)),
        'ffn2_w_gate': nrm((DEPTH, d, D_FF), d ** -0.5),
        'ffn2_w_up': nrm((DEPTH, d, D_FF), d ** -0.5),
        'ffn2_w_down': nrm((DEPTH, D_FF, d), D_FF ** -0.5),
        'ple_norm': gain((DEPTH, d)),
        'ple_w_gate': nrm((DEPTH, d, d), d ** -0.5),
        'ple_w_proj': nrm((DEPTH, D_PLE, d), D_PLE ** -0.5),
        'ple_post_norm': gain((DEPTH, d)),
    }


def reference(x_prompt, x_sample, cache_a_k, cache_a_v, cache_a_kidx, state_b, state_c, state_c_conv,
              p_prompt, p_sample,
              ffn1_norm, ffn1_w_gate, ffn1_w_up, ffn1_w_down, mix_norm, even_w_in, even_w_out,
              a_q_norm, a_k_norm, b_lb_logits, b_out_norm, odd_w_in, odd_w_out, c_conv_w, c_a_log,
              c_dt_bias, c_out_norm, ffn2_norm, ffn2_w_gate, ffn2_w_up, ffn2_w_down,
              ple_norm, ple_w_gate, ple_w_proj, ple_post_norm):
    f32 = jnp.float32
    lb_soft = jax.nn.softmax(b_lb_logits.astype(f32), axis=0)
    lbs = jnp.cumsum(lb_soft, axis=0) - lb_soft[0]

    def trunk(x, p, a_k_past, a_v_past, a_ki_past, b_past, c_past, conv_past):
        bsz, t, _ = x.shape
        past_len = 0 if a_k_past is None else a_k_past.shape[2]
        q_pos = past_len + jnp.arange(t, dtype=jnp.int32)
        new_k, new_v, new_ki, new_b, new_c, new_conv = [], [], [], [], [], []
        for layer in range(DEPTH):
            x = x + 0.5 * swiglu(rms_norm(x, ffn1_norm[layer]), ffn1_w_gate[layer],
                                 ffn1_w_up[layer], ffn1_w_down[layer])
            h = rms_norm(x, mix_norm[layer])
            j = layer // 2
            if layer % 2 == 0:
                proj = h @ even_w_in[j]
                aq, ak, av, iq, ik, iw, bq, bf, bi, bg = split_cols(proj, EVEN_COLS)
                aq = rope(rms_norm(aq.reshape(bsz, t, A_HEADS, HEAD_DIM), a_q_norm[j]), q_pos, HEAD_DIM)
                ak = rope(rms_norm(ak.reshape(bsz, t, A_KV_HEADS, HEAD_DIM), a_k_norm[j]), q_pos, HEAD_DIM)
                av = av.reshape(bsz, t, A_KV_HEADS, HEAD_DIM)
                iq = rope(iq.reshape(bsz, t, IDX_HEADS, IDX_DIM), q_pos, IDX_ROPE_DIM)
                ik = rope(ik.reshape(bsz, t, 1, IDX_DIM), q_pos, IDX_ROPE_DIM)[:, :, 0]
                new_k.append(ak)
                new_v.append(av)
                new_ki.append(ik)
                if a_k_past is None:
                    keys, vals, ikeys = ak, av, ik
                    s0_b = jnp.zeros((bsz, B_HEADS, B_DK, B_DV), f32)
                else:
                    keys = jnp.concatenate([a_k_past[j].astype(ak.dtype), ak], axis=1)
                    vals = jnp.concatenate([a_v_past[j].astype(av.dtype), av], axis=1)
                    ikeys = jnp.concatenate([a_ki_past[j].astype(ik.dtype), ik], axis=1)
                    s0_b = b_past[j].astype(f32)
                k_pos = jnp.arange(keys.shape[1], dtype=jnp.int32)
                o_a = dsa_attention(aq, keys, vals, iq, ikeys, iw, q_pos, k_pos)
                o_b, s_b = hgrn2_mixer(bq, bf, bi, bg, lbs[j], b_out_norm[j], s0_b)
                new_b.append(s_b)
                mixed = jnp.concatenate([o_a.reshape(bsz, t, A_HEADS * HEAD_DIM).astype(x.dtype),
                                         o_b.astype(x.dtype)], axis=-1) @ even_w_out[j]
            else:
                proj = h @ odd_w_in[j]
                qkv, z, b_raw, a_raw = split_cols(proj, ODD_COLS)
                if conv_past is None:
                    conv_state = jnp.zeros((bsz, C_CONV - 1, C_CONV_CH), qkv.dtype)
                    s0_c = jnp.zeros((bsz, C_V_HEADS, C_DK, C_DV), f32)
                else:
                    conv_state = conv_past[j].astype(qkv.dtype)
                    s0_c = c_past[j].astype(f32)
                xpad = jnp.concatenate([conv_state, qkv], axis=1)
                conv = sum(xpad[:, tap:tap + t] * c_conv_w[j][tap] for tap in range(C_CONV))
                new_conv.append(xpad[:, -(C_CONV - 1):])
                act = jax.nn.silu(conv.astype(f32))
                cq, ck, cv = split_cols(act, (C_QK_HEADS * C_DK, C_QK_HEADS * C_DK, C_V_HEADS * C_DV))
                rep = C_V_HEADS // C_QK_HEADS
                cq = jnp.repeat(l2norm(cq.reshape(bsz, t, C_QK_HEADS, C_DK)), rep, axis=2) * (C_DK ** -0.5)
                ck = jnp.repeat(l2norm(ck.reshape(bsz, t, C_QK_HEADS, C_DK)), rep, axis=2)
                cv = cv.reshape(bsz, t, C_V_HEADS, C_DV)
                beta = jax.nn.sigmoid(b_raw.astype(f32))
                g = -jnp.exp(c_a_log[j].astype(f32)) * jax.nn.softplus(a_raw.astype(f32) + c_dt_bias[j].astype(f32))
                o_c, s_c = gated_delta_chunked(cq, ck, cv, g, beta, s0_c)
                new_c.append(s_c)
                o_c = rms_norm(o_c, c_out_norm[j]) * jax.nn.silu(z.astype(f32).reshape(bsz, t, C_V_HEADS, C_DV))
                mixed = o_c.reshape(bsz, t, ODD_OUT).astype(x.dtype) @ odd_w_out[j]
            x = x + mixed
            x = x + 0.5 * swiglu(rms_norm(x, ffn2_norm[layer]), ffn2_w_gate[layer],
                                 ffn2_w_up[layer], ffn2_w_down[layer])
            gate = jax.nn.sigmoid(rms_norm(x, ple_norm[layer]) @ ple_w_gate[layer])
            x = x + gate * rms_norm(p[layer] @ ple_w_proj[layer], ple_post_norm[layer])
        return x, (jnp.stack(new_k), jnp.stack(new_v), jnp.stack(new_ki),
                   jnp.stack(new_b), jnp.stack(new_c), jnp.stack(new_conv))

    y_prompt, (pk, pv, pki, pb, pc, pconv) = trunk(x_prompt, p_prompt, None, None, None, None, None, None)
    y_sample, (sk, sv, ski, sb, sc, sconv) = trunk(x_sample, p_sample, cache_a_k, cache_a_v, cache_a_kidx,
                                                   state_b, state_c, state_c_conv)
    return (y_prompt, y_sample, pk, pv, pki, pb, pc, pconv, sk, sv, ski, sb, sc, sconv)
```

```python
from functools import partial

import jax
import jax.numpy as jnp
from jax import lax
from jax.experimental import pallas as pl
from jax.experimental.pallas import tpu as pltpu

D_MODEL = 4096
DEPTH = 4
CHUNK = 64
QBLOCK = 128
EPS = 1e-6
NEG_BIG = -1e30
ROPE_THETA = 10000.0
HEAD_DIM = 128
A_HEADS = 16
A_KV_HEADS = 4
IDX_HEADS = 32
IDX_DIM = 128
IDX_ROPE_DIM = 64
TOPK_MAX = 256
IDX_SCALE = (IDX_HEADS * IDX_DIM) ** -0.5
B_HEADS = 16
B_DK = 128
B_DV = 128
C_QK_HEADS = 16
C_V_HEADS = 32
C_DK = 128
C_DV = 128
C_CONV = 4
C_CONV_CH = 2 * C_QK_HEADS * C_DK + C_V_HEADS * C_DV

V7X_VMEM_LIMIT_BYTES = 56 * 1024 * 1024

BF16 = jnp.bfloat16
F32 = jnp.float32


def _params(*sem):
    return pltpu.CompilerParams(dimension_semantics=sem,
                                vmem_limit_bytes=V7X_VMEM_LIMIT_BYTES)


def _rmsnorm_body(x_ref, w_ref, o_ref):
    x = x_ref[...]
    ms = jnp.mean(x * x, axis=-1, keepdims=True)
    o_ref[...] = (x * lax.rsqrt(ms + EPS) * w_ref[...]).astype(o_ref.dtype)


def rmsnorm(x, w, tm=512, out_dtype=BF16):
    m, d = x.shape
    return pl.pallas_call(
        _rmsnorm_body,
        grid=(m // tm,),
        in_specs=[pl.BlockSpec((tm, d), lambda i: (i, 0)),
                  pl.BlockSpec((1, d), lambda i: (0, 0))],
        out_specs=pl.BlockSpec((tm, d), lambda i: (i, 0)),
        out_shape=jax.ShapeDtypeStruct((m, d), out_dtype),
        compiler_params=_params("parallel"),
        name="rmsnorm",
    )(x, w.reshape(1, d).astype(F32))


def _mm_body(a_ref, w_ref, o_ref):
    o_ref[...] = jnp.dot(a_ref[...], w_ref[...], preferred_element_type=F32)


def _mm_resid_body(a_ref, w_ref, r_ref, o_ref):
    o_ref[...] = r_ref[...] + jnp.dot(a_ref[...], w_ref[...], preferred_element_type=F32)


def _mm_gate_body(a_ref, w_ref, r_ref, p_ref, o_ref):
    acc = jnp.dot(a_ref[...], w_ref[...], preferred_element_type=F32)
    o_ref[...] = r_ref[...] + jax.nn.sigmoid(acc) * p_ref[...]


def matmul(a, w, *extras, tm=1024, tn=512, body=_mm_body, name="matmul"):
    m, k = a.shape
    n = w.shape[1]
    tile = pl.BlockSpec((tm, tn), lambda i, j: (i, j))
    return pl.pallas_call(
        body,
        grid=(m // tm, n // tn),
        in_specs=[pl.BlockSpec((tm, k), lambda i, j: (i, 0)),
                  pl.BlockSpec((k, tn), lambda i, j: (0, j))] + [tile] * len(extras),
        out_specs=tile,
        out_shape=jax.ShapeDtypeStruct((m, n), F32),
        compiler_params=_params("parallel", "arbitrary"),
        name=name,
    )(a, w, *extras)


def _ple_proj_body(p_ref, w_ref, nw_ref, o_ref):
    y = jnp.dot(p_ref[...], w_ref[...], preferred_element_type=F32)
    ms = jnp.mean(y * y, axis=-1, keepdims=True)
    o_ref[...] = y * lax.rsqrt(ms + EPS) * nw_ref[...]


def ple_proj(p, w, nw, tm=512):
    m, k = p.shape
    n = w.shape[1]
    return pl.pallas_call(
        _ple_proj_body,
        grid=(m // tm,),
        in_specs=[pl.BlockSpec((tm, k), lambda i: (i, 0)),
                  pl.BlockSpec((k, n), lambda i: (0, 0)),
                  pl.BlockSpec((1, n), lambda i: (0, 0))],
        out_specs=pl.BlockSpec((tm, n), lambda i: (i, 0)),
        out_shape=jax.ShapeDtypeStruct((m, n), F32),
        compiler_params=_params("parallel"),
        name="ple_proj",
    )(p, w, nw.reshape(1, n).astype(F32))


def _ffn_body(x_ref, nw_ref, wg_ref, wu_ref, wd_ref, o_ref, h_ref):
    f = pl.program_id(1)

    @pl.when(f == 0)
    def _():
        x = x_ref[...]
        ms = jnp.mean(x * x, axis=-1, keepdims=True)
        h_ref[...] = (x * lax.rsqrt(ms + EPS) * nw_ref[...]).astype(BF16)

        o_ref[...] = jnp.zeros_like(o_ref)

    h = h_ref[...]
    g = jnp.dot(h, wg_ref[...], preferred_element_type=F32)
    u = jnp.dot(h, wu_ref[...], preferred_element_type=F32)
    act = (g * jax.nn.sigmoid(g) * u).astype(BF16)
    o_ref[...] += jnp.dot(act, wd_ref[...], preferred_element_type=F32)

    @pl.when(f == pl.num_programs(1) - 1)
    def _():
        o_ref[...] = x_ref[...] + 0.5 * o_ref[...]


def ffn(x, nw, wg, wu, wd, tm=512, tf=256):
    m, d = x.shape
    dff = wg.shape[1]
    return pl.pallas_call(
        _ffn_body,
        grid=(m // tm, dff // tf),
        in_specs=[pl.BlockSpec((tm, d), lambda i, f: (i, 0), pipeline_mode=pl.Buffered(1)),
                  pl.BlockSpec((1, d), lambda i, f: (0, 0)),
                  pl.BlockSpec((d, tf), lambda i, f: (0, f)),
                  pl.BlockSpec((d, tf), lambda i, f: (0, f)),
                  pl.BlockSpec((tf, d), lambda i, f: (f, 0))],
        out_specs=pl.BlockSpec((tm, d), lambda i, f: (i, 0)),
        out_shape=jax.ShapeDtypeStruct((m, d), F32),
        scratch_shapes=[pltpu.VMEM((tm, d), BF16)],
        compiler_params=_params("parallel", "arbitrary"),
        name="ffn",
    )(x, nw.reshape(1, d).astype(F32), wg, wu, wd)


def _rms_f32(x, w):
    y = x * lax.rsqrt(jnp.mean(x * x, axis=-1, keepdims=True) + EPS)
    return y * w.astype(F32)


def _l2norm(x):
    return x * lax.rsqrt(jnp.sum(x * x, axis=-1, keepdims=True) + EPS)


def _rope(x, pos, rot_dim):
    half = rot_dim // 2
    inv_freq = ROPE_THETA ** (-jnp.arange(half, dtype=F32) / half)
    ang = pos.astype(F32)[:, None] * inv_freq[None, :]
    cos = jnp.cos(ang)[None, :, None, :]
    sin = jnp.sin(ang)[None, :, None, :]
    x1, x2, rest = x[..., :half], x[..., half:rot_dim], x[..., rot_dim:]
    return jnp.concatenate([x1 * cos - x2 * sin, x2 * cos + x1 * sin, rest], axis=-1)


def _dsa_attention(q, k, v, qi, ki, wi, q_pos, k_pos):
    b, t = q.shape[:2]
    l = k.shape[1]
    topk = min(TOPK_MAX, l // 4)
    qb = min(QBLOCK, t)
    nb = t // qb
    grp = A_HEADS // A_KV_HEADS
    k_chunk = k_pos // CHUNK
    scale = HEAD_DIM ** -0.5

    def blocks(a):
        return jnp.moveaxis(a.reshape((b, nb, qb) + a.shape[2:]), 1, 0)

    def one_block(args):
        q_b, qi_b, wi_b, pos_b = args
        q_chunk = pos_b // CHUNK
        logits = jnp.einsum('bqhd,bkd->bqhk', qi_b, ki).astype(F32)
        score = jnp.einsum('bqh,bqhk->bqk', wi_b, jax.nn.relu(logits)) * IDX_SCALE
        admissible = k_chunk[None, :] <= q_chunk[:, None]
        score = jnp.where(admissible[None], score, NEG_BIG)
        _, top_idx = lax.top_k(score, topk)
        valid = k_chunk[top_idx] <= q_chunk[None, :, None]
        kg = jax.vmap(lambda kk, ii: kk[ii])(k, top_idx)
        vg = jax.vmap(lambda vv, ii: vv[ii])(v, top_idx)
        qg = q_b.reshape(b, qb, A_KV_HEADS, grp, HEAD_DIM)
        s = jnp.einsum('bqngd,bqknd->bqngk', qg, kg).astype(F32) * scale
        s = jnp.where(valid[:, :, None, None, :], s, NEG_BIG)
        p = jax.nn.softmax(s, axis=-1)
        o = jnp.einsum('bqngk,bqknd->bqngd', p, vg)
        return o.reshape(b, qb, A_HEADS, HEAD_DIM)

    out = lax.map(one_block, (blocks(q), blocks(qi), blocks(wi), q_pos.reshape(nb, qb)))
    return jnp.moveaxis(out, 0, 1).reshape(b, t, A_HEADS, HEAD_DIM)


def _gla_chunked(q, k, g, v, s0):
    b, t, h, dk = q.shape
    dv = v.shape[-1]
    c = min(CHUNK, t)
    n = t // c
    causal = jnp.tril(jnp.ones((c, c), dtype=bool))[None, :, :, None, None]

    def cm(a):
        return jnp.moveaxis(a.reshape((b, n, c) + a.shape[2:]), 1, 0)

    def step(s, xs):
        qc, kc, gc, vc = xs
        gcum = jnp.cumsum(gc, axis=1)
        o_inter = jnp.einsum('bthc,bhcv->bthv', qc * jnp.exp(gcum), s)
        diff = gcum[:, :, None] - gcum[:, None, :]
        w = jnp.where(causal, jnp.exp(jnp.where(causal, diff, 0.0)), 0.0)
        att = jnp.einsum('bthc,bshc,btshc->bths', qc, kc, w)
        o_intra = jnp.einsum('bths,bshv->bthv', att, vc)
        g_last = gcum[:, -1]
        s_new = jnp.exp(g_last)[..., None] * s + jnp.einsum(
            'bshc,bshv->bhcv', kc * jnp.exp(g_last[:, None] - gcum), vc)
        return s_new, o_inter + o_intra

    s_fin, o = lax.scan(step, s0, (cm(q), cm(k), cm(g), cm(v)))
    return jnp.moveaxis(o, 0, 1).reshape(b, t, h, dv), s_fin


def _gated_delta_chunked(q, k, v, g, beta, s0):
    b, t, h, dk = q.shape
    dv = v.shape[-1]
    c = min(CHUNK, t)
    n = t // c

    def heads_first(a):
        a = a.reshape((b, n, c, h) + a.shape[3:])
        return jnp.moveaxis(a, 3, 2)

    qh, kh, vh, gh, bh = [heads_first(a) for a in (q, k, v, g, beta)]
    gcum = jnp.cumsum(gh, axis=-1)
    diff = gcum[..., :, None] - gcum[..., None, :]
    incl = jnp.tril(jnp.ones((c, c), dtype=bool))
    strict = jnp.tril(jnp.ones((c, c), dtype=bool), -1)
    decay = jnp.where(incl, jnp.exp(jnp.where(incl, diff, 0.0)), 0.0)
    kb = kh * bh[..., None]
    a_mat = jnp.where(strict, jnp.einsum('bnhtd,bnhsd->bnhts', kb, kh) * decay, 0.0)
    rhs = jnp.concatenate([vh * bh[..., None], kb * jnp.exp(gcum)[..., None]], axis=-1)
    sol = lax.linalg.triangular_solve(a_mat + jnp.eye(c, dtype=a_mat.dtype), rhs,
                                      left_side=True, lower=True, unit_diagonal=True)
    u0, w = sol[..., :dv], sol[..., dv:]
    qk = jnp.einsum('bnhtd,bnhsd->bnhts', qh, kh) * decay
    q_dec = qh * jnp.exp(gcum)[..., None]
    g_last = gcum[..., -1]
    k_dec = kh * jnp.exp(g_last[..., None] - gcum)[..., None]

    def step(s, xs):
        u0c, wc, qkc, qdc, kdc, glc = xs
        u = u0c - jnp.einsum('bhtk,bhkv->bhtv', wc, s)
        o = jnp.einsum('bhtk,bhkv->bhtv', qdc, s) + jnp.einsum('bhts,bhsv->bhtv', qkc, u)
        s = jnp.exp(glc)[..., None, None] * s + jnp.einsum('bhsk,bhsv->bhkv', kdc, u)
        return s, o

    xs = tuple(jnp.moveaxis(a, 1, 0) for a in (u0, w, qk, q_dec, k_dec, g_last))
    s_fin, o = lax.scan(step, s0, xs)
    o = jnp.transpose(o, (1, 0, 3, 2, 4)).reshape(b, t, h, dv)
    return o, s_fin


def _hgrn2_mixer(q_raw, f_raw, i_raw, g_raw, lb, norm_w, s0):
    b, t, _ = q_raw.shape
    k = (1.0 - lb) * jax.nn.sigmoid(-f_raw)
    log_f = jnp.log1p(-jnp.minimum(k, 1.0 - 1e-6))
    q = jax.nn.silu(q_raw)
    shp = (b, t, B_HEADS, B_DK)
    o, s_fin = _gla_chunked(q.reshape(shp), k.reshape(shp), log_f.reshape(shp),
                            i_raw.reshape(b, t, B_HEADS, B_DV), s0)
    o = _rms_f32(o, norm_w) * jax.nn.silu(g_raw.reshape(b, t, B_HEADS, B_DV))
    return o.reshape(b, t, B_HEADS * B_DV), s_fin


def _even_mixer(proj_a, proj_b, bsz, t, past, aqn, akn, lb, bnorm):
    pa = proj_a.reshape(bsz, t, -1)
    pb = proj_b.reshape(bsz, t, -1)
    aq, ak, av = pa[..., :2048], pa[..., 2048:2560], pa[..., 2560:3072]
    iq, ik, iw = pa[..., 3072:7168], pa[..., 7168:7296], pa[..., 7296:7328]
    bq, bf, bi, bg = [pb[..., i * 2048:(i + 1) * 2048] for i in range(4)]
    past_len = 0 if past is None else past[0].shape[1]
    q_pos = past_len + jnp.arange(t, dtype=jnp.int32)
    aq = _rope(_rms_f32(aq.reshape(bsz, t, A_HEADS, HEAD_DIM), aqn), q_pos, HEAD_DIM)
    ak = _rope(_rms_f32(ak.reshape(bsz, t, A_KV_HEADS, HEAD_DIM), akn), q_pos, HEAD_DIM)
    av = av.reshape(bsz, t, A_KV_HEADS, HEAD_DIM)
    iq = _rope(iq.reshape(bsz, t, IDX_HEADS, IDX_DIM), q_pos, IDX_ROPE_DIM)
    ik = _rope(ik.reshape(bsz, t, 1, IDX_DIM), q_pos, IDX_ROPE_DIM)[:, :, 0]
    if past is None:
        keys, vals, ikeys = ak, av, ik
        s0_b = jnp.zeros((bsz, B_HEADS, B_DK, B_DV), F32)
    else:
        keys = jnp.concatenate([past[0], ak], axis=1)
        vals = jnp.concatenate([past[1], av], axis=1)
        ikeys = jnp.concatenate([past[2], ik], axis=1)
        s0_b = past[3]
    k_pos = jnp.arange(keys.shape[1], dtype=jnp.int32)
    o_a = _dsa_attention(aq, keys, vals, iq, ikeys, iw, q_pos, k_pos)
    o_b, s_b = _hgrn2_mixer(bq, bf, bi, bg, lb, bnorm, s0_b)
    mixed_in = jnp.concatenate([o_a.reshape(bsz, t, A_HEADS * HEAD_DIM), o_b], axis=-1)
    return mixed_in.reshape(bsz * t, -1), (ak, av, ik, s_b)


def _odd_mixer(proj_m, proj_s, bsz, t, past, conv_w, a_log, dt_bias, onorm):
    pm = proj_m.reshape(bsz, t, -1)
    ps = proj_s.reshape(bsz, t, -1)
    qkv, z = pm[..., :C_CONV_CH], pm[..., C_CONV_CH:]
    b_raw, a_raw = ps[..., :C_V_HEADS], ps[..., C_V_HEADS:2 * C_V_HEADS]
    if past is None:
        conv_state = jnp.zeros((bsz, C_CONV - 1, C_CONV_CH), F32)
        s0_c = jnp.zeros((bsz, C_V_HEADS, C_DK, C_DV), F32)
    else:
        s0_c, conv_state = past
    xpad = jnp.concatenate([conv_state, qkv], axis=1)
    conv = sum(xpad[:, tap:tap + t] * conv_w[tap] for tap in range(C_CONV))
    new_conv = xpad[:, -(C_CONV - 1):]
    act = jax.nn.silu(conv)
    nqk = C_QK_HEADS * C_DK
    cq, ck, cv = act[..., :nqk], act[..., nqk:2 * nqk], act[..., 2 * nqk:]
    rep = C_V_HEADS // C_QK_HEADS
    cq = jnp.repeat(_l2norm(cq.reshape(bsz, t, C_QK_HEADS, C_DK)), rep, axis=2) * (C_DK ** -0.5)
    ck = jnp.repeat(_l2norm(ck.reshape(bsz, t, C_QK_HEADS, C_DK)), rep, axis=2)
    cv = cv.reshape(bsz, t, C_V_HEADS, C_DV)
    beta = jax.nn.sigmoid(b_raw)
    g = -jnp.exp(a_log.astype(F32)) * jax.nn.softplus(a_raw + dt_bias.astype(F32))
    o_c, s_c = _gated_delta_chunked(cq, ck, cv, g, beta, s0_c)
    o_c = _rms_f32(o_c, onorm) * jax.nn.silu(z.reshape(bsz, t, C_V_HEADS, C_DV))
    return o_c.reshape(bsz * t, -1), (s_c, new_conv)


def _pad_cols(w, n):
    return jnp.pad(w, ((0, 0), (0, n - w.shape[1])))


def kernel(x_prompt, x_sample, cache_a_k, cache_a_v, cache_a_kidx, state_b, state_c, state_c_conv, p_prompt, p_sample, ffn1_norm, ffn1_w_gate, ffn1_w_up, ffn1_w_down, mix_norm, even_w_in, even_w_out, a_q_norm, a_k_norm, b_lb_logits, b_out_norm, odd_w_in, odd_w_out, c_conv_w, c_a_log, c_dt_bias, c_out_norm, ffn2_norm, ffn2_w_gate, ffn2_w_up, ffn2_w_down, ple_norm, ple_w_gate, ple_w_proj, ple_post_norm):
    d = D_MODEL
    bp, tp = x_prompt.shape[:2]
    bs, ts = x_sample.shape[:2]
    mp, ms = bp * tp, bs * ts
    x = jnp.concatenate([x_prompt.reshape(mp, d), x_sample.reshape(ms, d)], axis=0)
    p_all = jnp.concatenate([p_prompt.reshape(DEPTH, mp, -1), p_sample.reshape(DEPTH, ms, -1)],
                            axis=1).astype(BF16)

    lb_soft = jax.nn.softmax(b_lb_logits.astype(F32), axis=0)
    lbs = jnp.cumsum(lb_soft, axis=0) - lb_soft[0]

    new_p = [[] for _ in range(6)]
    new_s = [[] for _ in range(6)]
    for layer in range(DEPTH):
        j = layer // 2
        x = ffn(x, ffn1_norm[layer], ffn1_w_gate[layer].astype(BF16),
                ffn1_w_up[layer].astype(BF16), ffn1_w_down[layer].astype(BF16))
        h = rmsnorm(x, mix_norm[layer])
        if layer % 2 == 0:
            w_in = even_w_in[j]
            w_a = _pad_cols(w_in[:, :7328], 7680).astype(BF16)
            w_b = w_in[:, 7328:].astype(BF16)
            proj_a = matmul(h, w_a, name="even_in_a")
            proj_b = matmul(h, w_b, name="even_in_b")
            mix_p, st_p = _even_mixer(proj_a[:mp], proj_b[:mp], bp, tp, None,
                                      a_q_norm[j], a_k_norm[j], lbs[j], b_out_norm[j])
            mix_s, st_s = _even_mixer(proj_a[mp:], proj_b[mp:], bs, ts,
                                      (cache_a_k[j], cache_a_v[j], cache_a_kidx[j], state_b[j]),
                                      a_q_norm[j], a_k_norm[j], lbs[j], b_out_norm[j])
            for idx in range(4):
                new_p[idx].append(st_p[idx])
                new_s[idx].append(st_s[idx])
            w_out = even_w_out[j].astype(BF16)
        else:
            w_in = odd_w_in[j]
            w_m = w_in[:, :12288].astype(BF16)
            w_s = _pad_cols(w_in[:, 12288:], 128).astype(BF16)
            proj_m = matmul(h, w_m, name="odd_in_m")
            proj_s = matmul(h, w_s, tn=128, name="odd_in_s")
            mix_p, st_p = _odd_mixer(proj_m[:mp], proj_s[:mp], bp, tp, None,
                                     c_conv_w[j], c_a_log[j], c_dt_bias[j], c_out_norm[j])
            mix_s, st_s = _odd_mixer(proj_m[mp:], proj_s[mp:], bs, ts,
                                     (state_c[j], state_c_conv[j]),
                                     c_conv_w[j], c_a_log[j], c_dt_bias[j], c_out_norm[j])
            for idx in range(2):
                new_p[4 + idx].append(st_p[idx])
                new_s[4 + idx].append(st_s[idx])
            w_out = odd_w_out[j].astype(BF16)
        mix_in = jnp.concatenate([mix_p, mix_s], axis=0).astype(BF16)
        x = matmul(mix_in, w_out, x, body=_mm_resid_body, name="mix_out")
        x = ffn(x, ffn2_norm[layer], ffn2_w_gate[layer].astype(BF16),
                ffn2_w_up[layer].astype(BF16), ffn2_w_down[layer].astype(BF16))
        hp = rmsnorm(x, ple_norm[layer])
        pp = ple_proj(p_all[layer], ple_w_proj[layer].astype(BF16), ple_post_norm[layer])
        x = matmul(hp, ple_w_gate[layer].astype(BF16), x, pp, body=_mm_gate_body, name="ple_gate")

    y_prompt = x[:mp].reshape(bp, tp, d)
    y_sample = x[mp:].reshape(bs, ts, d)
    outs_p = tuple(jnp.stack(v) for v in new_p)
    outs_s = tuple(jnp.stack(v) for v in new_s)
    return (y_prompt, y_sample) + outs_p + outs_s
```

```python
from functools import partial

import jax
import jax.numpy as jnp
from jax import lax
from jax.experimental import pallas as pl
from jax.experimental.pallas import tpu as pltpu

D_MODEL = 4096
DEPTH = 4
CHUNK = 64
QBLOCK = 128
EPS = 1e-6
NEG_BIG = -1e30
ROPE_THETA = 10000.0
HEAD_DIM = 128
A_HEADS = 16
A_KV_HEADS = 4
IDX_HEADS = 32
IDX_DIM = 128
IDX_ROPE_DIM = 64
TOPK_MAX = 256
IDX_SCALE = (IDX_HEADS * IDX_DIM) ** -0.5
B_HEADS = 16
B_DK = 128
B_DV = 128
C_QK_HEADS = 16
C_V_HEADS = 32
C_DK = 128
C_DV = 128
C_CONV = 4
C_CONV_CH = 2 * C_QK_HEADS * C_DK + C_V_HEADS * C_DV

V7X_VMEM_LIMIT_BYTES = 56 * 1024 * 1024

BF16 = jnp.bfloat16
F32 = jnp.float32


def _params(*sem):
    return pltpu.CompilerParams(dimension_semantics=sem,
                                vmem_limit_bytes=V7X_VMEM_LIMIT_BYTES)


def _rmsnorm_body(x_ref, w_ref, o_ref):
    x = x_ref[...]
    ms = jnp.mean(x * x, axis=-1, keepdims=True)
    o_ref[...] = (x * lax.rsqrt(ms + EPS) * w_ref[...]).astype(o_ref.dtype)


def rmsnorm(x, w, tm=512, out_dtype=BF16):
    m, d = x.shape
    return pl.pallas_call(
        _rmsnorm_body,
        grid=(m // tm,),
        in_specs=[pl.BlockSpec((tm, d), lambda i: (i, 0)),
                  pl.BlockSpec((1, d), lambda i: (0, 0))],
        out_specs=pl.BlockSpec((tm, d), lambda i: (i, 0)),
        out_shape=jax.ShapeDtypeStruct((m, d), out_dtype),
        compiler_params=_params("parallel"),
        name="rmsnorm",
    )(x, w.reshape(1, d).astype(F32))


def _mm_body(a_ref, w_ref, o_ref):
    o_ref[...] = jnp.dot(a_ref[...], w_ref[...], preferred_element_type=F32)


def _mm_resid_body(a_ref, w_ref, r_ref, o_ref):
    o_ref[...] = r_ref[...] + jnp.dot(a_ref[...], w_ref[...], preferred_element_type=F32)


def _mm_gate_body(a_ref, w_ref, r_ref, p_ref, o_ref):
    acc = jnp.dot(a_ref[...], w_ref[...], preferred_element_type=F32)
    o_ref[...] = r_ref[...] + jax.nn.sigmoid(acc) * p_ref[...]


def matmul(a, w, *extras, tm=1024, tn=512, body=_mm_body, name="matmul"):
    m, k = a.shape
    n = w.shape[1]
    tile = pl.BlockSpec((tm, tn), lambda i, j: (i, j))
    return pl.pallas_call(
        body,
        grid=(m // tm, n // tn),
        in_specs=[pl.BlockSpec((tm, k), lambda i, j: (i, 0)),
                  pl.BlockSpec((k, tn), lambda i, j: (0, j))] + [tile] * len(extras),
        out_specs=tile,
        out_shape=jax.ShapeDtypeStruct((m, n), F32),
        compiler_params=_params("parallel", "arbitrary"),
        name=name,
    )(a, w, *extras)


def _ple_proj_body(p_ref, w_ref, nw_ref, o_ref):
    y = jnp.dot(p_ref[...], w_ref[...], preferred_element_type=F32)
    ms = jnp.mean(y * y, axis=-1, keepdims=True)
    o_ref[...] = y * lax.rsqrt(ms + EPS) * nw_ref[...]


def ple_proj(p, w, nw, tm=512):
    m, k = p.shape
    n = w.shape[1]
    return pl.pallas_call(
        _ple_proj_body,
        grid=(m // tm,),
        in_specs=[pl.BlockSpec((tm, k), lambda i: (i, 0)),
                  pl.BlockSpec((k, n), lambda i: (0, 0)),
                  pl.BlockSpec((1, n), lambda i: (0, 0))],
        out_specs=pl.BlockSpec((tm, n), lambda i: (i, 0)),
        out_shape=jax.ShapeDtypeStruct((m, n), F32),
        compiler_params=_params("parallel"),
        name="ple_proj",
    )(p, w, nw.reshape(1, n).astype(F32))


def _ffn_body(x_ref, nw_ref, wg_ref, wu_ref, wd_ref, o_ref, h_ref):
    f = pl.program_id(1)

    @pl.when(f == 0)
    def _():
        x = x_ref[...]
        ms = jnp.mean(x * x, axis=-1, keepdims=True)
        h_ref[...] = (x * lax.rsqrt(ms + EPS) * nw_ref[...]).astype(BF16)

        o_ref[...] = jnp.zeros_like(o_ref)

    h = h_ref[...]
    g = jnp.dot(h, wg_ref[...], preferred_element_type=F32)
    u = jnp.dot(h, wu_ref[...], preferred_element_type=F32)
    act = (g * jax.nn.sigmoid(g) * u).astype(BF16)
    o_ref[...] += jnp.dot(act, wd_ref[...], preferred_element_type=F32)

    @pl.when(f == pl.num_programs(1) - 1)
    def _():
        o_ref[...] = x_ref[...] + 0.5 * o_ref[...]


def ffn(x, nw, wg, wu, wd, tm=512, tf=256):
    m, d = x.shape
    dff = wg.shape[1]
    return pl.pallas_call(
        _ffn_body,
        grid=(m // tm, dff // tf),
        in_specs=[pl.BlockSpec((tm, d), lambda i, f: (i, 0), pipeline_mode=pl.Buffered(1)),
                  pl.BlockSpec((1, d), lambda i, f: (0, 0)),
                  pl.BlockSpec((d, tf), lambda i, f: (0, f)),
                  pl.BlockSpec((d, tf), lambda i, f: (0, f)),
                  pl.BlockSpec((tf, d), lambda i, f: (f, 0))],
        out_specs=pl.BlockSpec((tm, d), lambda i, f: (i, 0)),
        out_shape=jax.ShapeDtypeStruct((m, d), F32),
        scratch_shapes=[pltpu.VMEM((tm, d), BF16)],
        compiler_params=_params("parallel", "arbitrary"),
        name="ffn",
    )(x, nw.reshape(1, d).astype(F32), wg, wu, wd)


DSA_KEY_TILE = 256
DSA_HEAD_GROUP = 8
INT32_MIN = -2 ** 31
A_GROUP = A_HEADS // A_KV_HEADS
NT_DIMS = (((1,), (1,)), ((), ()))


def _dsa_body(qi_ref, wi_ref, q_ref, ki_ref, k_ref, v_ref, o_ref,
              qi_st, wib, skey, q_st, m_s, l_s, acc_s, *, qb, past_len, l_valid, topk):
    lk = DSA_KEY_TILE
    i = pl.program_id(1)
    q_pos0 = past_len + i * qb
    limit = jnp.minimum(((q_pos0 + qb - 1) // CHUNK + 1) * CHUNK, l_valid)
    nt = (limit + lk - 1) // lk

    for h in range(IDX_HEADS):
        qi_st[h * qb:(h + 1) * qb, :] = qi_ref[0, :, h * IDX_DIM:(h + 1) * IDX_DIM]
        wib[h] = jnp.broadcast_to(wi_ref[0, :, h:h + 1], (qb, 128))
    for n in range(A_KV_HEADS):
        for g in range(A_GROUP):
            hd = n * A_GROUP + g
            q_st[n, g * qb:(g + 1) * qb, :] = q_ref[0, :, hd * HEAD_DIM:(hd + 1) * HEAD_DIM]

    q_chunk = (q_pos0 + lax.broadcasted_iota(jnp.int32, (qb, 128), 0)) // CHUNK

    def score_tile(t, carry):
        kt = ki_ref[0, t]
        cols = [jnp.zeros((qb, 128), F32) for _ in range(lk // 128)]
        for hg in range(IDX_HEADS // DSA_HEAD_GROUP):
            rows = DSA_HEAD_GROUP * qb
            lg = lax.dot_general(qi_st[hg * rows:(hg + 1) * rows, :], kt, NT_DIMS,
                                 preferred_element_type=F32)
            for hh in range(DSA_HEAD_GROUP):
                w = wib[hg * DSA_HEAD_GROUP + hh]
                for c in range(lk // 128):
                    blk = lg[hh * qb:(hh + 1) * qb, c * 128:(c + 1) * 128]
                    cols[c] = cols[c] + jnp.maximum(blk, 0.0) * w
        for c in range(lk // 128):
            k_pos = t * lk + c * 128 + lax.broadcasted_iota(jnp.int32, (qb, 128), 1)
            adm = (k_pos // CHUNK <= q_chunk) & (k_pos < l_valid)
            bits = lax.bitcast_convert_type(cols[c], jnp.int32)
            key = jnp.where(bits < 0, bits ^ jnp.int32(0x7FFFFFFF), bits)
            skey[t, :, c * 128:(c + 1) * 128] = jnp.where(adm, key, jnp.int32(INT32_MIN))
        return carry

    lax.fori_loop(0, nt, score_tile, 0)

    def count_ge(cand):
        def body(t, acc):
            for c in range(lk // 128):
                acc = acc + jnp.where(skey[t, :, c * 128:(c + 1) * 128] >= cand, 1.0, 0.0)
            return acc
        acc = lax.fori_loop(0, nt, body, jnp.zeros((qb, 128), F32))
        return jnp.sum(acc, axis=1, keepdims=True)

    kf = jnp.float32(topk)
    zero = jnp.zeros((qb, 128), jnp.int32)
    prefix = jnp.where(count_ge(zero) >= kf, zero, jnp.int32(INT32_MIN))

    def bit_step(s, prefix):
        cand = prefix + lax.shift_left(jnp.int32(1), jnp.int32(30) - s)
        return jnp.where(count_ge(cand) >= kf, cand, prefix)

    prefix = lax.fori_loop(0, 31, bit_step, prefix)
    thr = jnp.maximum(prefix, jnp.int32(INT32_MIN + 1))

    m_s[...] = jnp.full(m_s.shape, NEG_BIG, F32)
    l_s[...] = jnp.zeros(l_s.shape, F32)
    acc_s[...] = jnp.zeros(acc_s.shape, F32)

    def attn_tile(t, carry):
        bias = jnp.concatenate(
            [jnp.where(skey[t, :, c * 128:(c + 1) * 128] >= thr, 0.0, NEG_BIG)
             for c in range(lk // 128)], axis=1)
        bias4 = jnp.concatenate([bias] * A_GROUP, axis=0)
        for n in range(A_KV_HEADS):
            kt = k_ref[0, t, :, n * HEAD_DIM:(n + 1) * HEAD_DIM]
            vt = v_ref[0, t, :, n * HEAD_DIM:(n + 1) * HEAD_DIM]
            s = lax.dot_general(q_st[n], kt, NT_DIMS, preferred_element_type=F32) + bias4
            m_old = m_s[n]
            m_new = jnp.maximum(m_old, jnp.max(s, axis=1, keepdims=True))
            alpha = jnp.exp(m_old - m_new)
            p = jnp.exp(s - m_new)
            l_s[n] = alpha * l_s[n] + jnp.sum(p, axis=1, keepdims=True)
            acc_s[n] = alpha * acc_s[n] + jnp.dot(p.astype(BF16), vt, preferred_element_type=F32)
            m_s[n] = m_new
        return carry

    lax.fori_loop(0, nt, attn_tile, 0)

    for n in range(A_KV_HEADS):
        o = acc_s[n] / l_s[n]
        for g in range(A_GROUP):
            hd = n * A_GROUP + g
            o_ref[0, :, hd * HEAD_DIM:(hd + 1) * HEAD_DIM] = o[g * qb:(g + 1) * qb, :].astype(o_ref.dtype)


def dsa_attention_pallas(q, k, v, qi, ki, wi, past_len):
    bsz, t, _ = q.shape
    l = k.shape[1]
    lk = DSA_KEY_TILE
    topk = min(TOPK_MAX, l // 4)
    qb = min(QBLOCK, t)
    n_tiles = -(-l // lk)
    pad = n_tiles * lk - l

    def tiles(a):
        a = jnp.pad(a, ((0, 0), (0, pad), (0, 0))).astype(BF16)
        return a.reshape(bsz, n_tiles, lk, a.shape[-1])

    qs = (q * (HEAD_DIM ** -0.5)).astype(BF16)
    whole = lambda shape: pl.BlockSpec(shape, lambda b, i: (b, 0, 0, 0), pipeline_mode=pl.Buffered(1))
    rows = lambda width: pl.BlockSpec((1, qb, width), lambda b, i: (b, i, 0))
    body = partial(_dsa_body, qb=qb, past_len=past_len, l_valid=l, topk=topk)
    return pl.pallas_call(
        body,
        grid=(bsz, t // qb),
        in_specs=[rows(IDX_HEADS * IDX_DIM), rows(IDX_HEADS), rows(A_HEADS * HEAD_DIM),
                  whole((1, n_tiles, lk, IDX_DIM)),
                  whole((1, n_tiles, lk, A_KV_HEADS * HEAD_DIM)),
                  whole((1, n_tiles, lk, A_KV_HEADS * HEAD_DIM))],
        out_specs=rows(A_HEADS * HEAD_DIM),
        out_shape=jax.ShapeDtypeStruct((bsz, t, A_HEADS * HEAD_DIM), BF16),
        scratch_shapes=[pltpu.VMEM((IDX_HEADS * qb, IDX_DIM), BF16),
                        pltpu.VMEM((IDX_HEADS, qb, 128), F32),
                        pltpu.VMEM((n_tiles, qb, lk), jnp.int32),
                        pltpu.VMEM((A_KV_HEADS, A_GROUP * qb, HEAD_DIM), BF16),
                        pltpu.VMEM((A_KV_HEADS, A_GROUP * qb, 1), F32),
                        pltpu.VMEM((A_KV_HEADS, A_GROUP * qb, 1), F32),
                        pltpu.VMEM((A_KV_HEADS, A_GROUP * qb, HEAD_DIM), F32)],
        compiler_params=_params("parallel", "arbitrary"),
        name="dsa_attention",
    )(qi.astype(BF16), wi * IDX_SCALE, qs, tiles(ki), tiles(k), tiles(v))


def _rms_f32(x, w):
    y = x * lax.rsqrt(jnp.mean(x * x, axis=-1, keepdims=True) + EPS)
    return y * w.astype(F32)


def _l2norm(x):
    return x * lax.rsqrt(jnp.sum(x * x, axis=-1, keepdims=True) + EPS)


def _rope(x, pos, rot_dim):
    half = rot_dim // 2
    inv_freq = ROPE_THETA ** (-jnp.arange(half, dtype=F32) / half)
    ang = pos.astype(F32)[:, None] * inv_freq[None, :]
    cos = jnp.cos(ang)[None, :, None, :]
    sin = jnp.sin(ang)[None, :, None, :]
    x1, x2, rest = x[..., :half], x[..., half:rot_dim], x[..., rot_dim:]
    return jnp.concatenate([x1 * cos - x2 * sin, x2 * cos + x1 * sin, rest], axis=-1)


def _dsa_attention(q, k, v, qi, ki, wi, q_pos, k_pos):
    b, t = q.shape[:2]
    l = k.shape[1]
    topk = min(TOPK_MAX, l // 4)
    qb = min(QBLOCK, t)
    nb = t // qb
    grp = A_HEADS // A_KV_HEADS
    k_chunk = k_pos // CHUNK
    scale = HEAD_DIM ** -0.5

    def blocks(a):
        return jnp.moveaxis(a.reshape((b, nb, qb) + a.shape[2:]), 1, 0)

    def one_block(args):
        q_b, qi_b, wi_b, pos_b = args
        q_chunk = pos_b // CHUNK
        logits = jnp.einsum('bqhd,bkd->bqhk', qi_b, ki).astype(F32)
        score = jnp.einsum('bqh,bqhk->bqk', wi_b, jax.nn.relu(logits)) * IDX_SCALE
        admissible = k_chunk[None, :] <= q_chunk[:, None]
        score = jnp.where(admissible[None], score, NEG_BIG)
        _, top_idx = lax.top_k(score, topk)
        valid = k_chunk[top_idx] <= q_chunk[None, :, None]
        kg = jax.vmap(lambda kk, ii: kk[ii])(k, top_idx)
        vg = jax.vmap(lambda vv, ii: vv[ii])(v, top_idx)
        qg = q_b.reshape(b, qb, A_KV_HEADS, grp, HEAD_DIM)
        s = jnp.einsum('bqngd,bqknd->bqngk', qg, kg).astype(F32) * scale
        s = jnp.where(valid[:, :, None, None, :], s, NEG_BIG)
        p = jax.nn.softmax(s, axis=-1)
        o = jnp.einsum('bqngk,bqknd->bqngd', p, vg)
        return o.reshape(b, qb, A_HEADS, HEAD_DIM)

    out = lax.map(one_block, (blocks(q), blocks(qi), blocks(wi), q_pos.reshape(nb, qb)))
    return jnp.moveaxis(out, 0, 1).reshape(b, t, A_HEADS, HEAD_DIM)


def _gla_chunked(q, k, g, v, s0):
    b, t, h, dk = q.shape
    dv = v.shape[-1]
    c = min(CHUNK, t)
    n = t // c
    causal = jnp.tril(jnp.ones((c, c), dtype=bool))[None, :, :, None, None]

    def cm(a):
        return jnp.moveaxis(a.reshape((b, n, c) + a.shape[2:]), 1, 0)

    def step(s, xs):
        qc, kc, gc, vc = xs
        gcum = jnp.cumsum(gc, axis=1)
        o_inter = jnp.einsum('bthc,bhcv->bthv', qc * jnp.exp(gcum), s)
        diff = gcum[:, :, None] - gcum[:, None, :]
        w = jnp.where(causal, jnp.exp(jnp.where(causal, diff, 0.0)), 0.0)
        att = jnp.einsum('bthc,bshc,btshc->bths', qc, kc, w)
        o_intra = jnp.einsum('bths,bshv->bthv', att, vc)
        g_last = gcum[:, -1]
        s_new = jnp.exp(g_last)[..., None] * s + jnp.einsum(
            'bshc,bshv->bhcv', kc * jnp.exp(g_last[:, None] - gcum), vc)
        return s_new, o_inter + o_intra

    s_fin, o = lax.scan(step, s0, (cm(q), cm(k), cm(g), cm(v)))
    return jnp.moveaxis(o, 0, 1).reshape(b, t, h, dv), s_fin


def _gated_delta_chunked(q, k, v, g, beta, s0):
    b, t, h, dk = q.shape
    dv = v.shape[-1]
    c = min(CHUNK, t)
    n = t // c

    def heads_first(a):
        a = a.reshape((b, n, c, h) + a.shape[3:])
        return jnp.moveaxis(a, 3, 2)

    qh, kh, vh, gh, bh = [heads_first(a) for a in (q, k, v, g, beta)]
    gcum = jnp.cumsum(gh, axis=-1)
    diff = gcum[..., :, None] - gcum[..., None, :]
    incl = jnp.tril(jnp.ones((c, c), dtype=bool))
    strict = jnp.tril(jnp.ones((c, c), dtype=bool), -1)
    decay = jnp.where(incl, jnp.exp(jnp.where(incl, diff, 0.0)), 0.0)
    kb = kh * bh[..., None]
    a_mat = jnp.where(strict, jnp.einsum('bnhtd,bnhsd->bnhts', kb, kh) * decay, 0.0)
    rhs = jnp.concatenate([vh * bh[..., None], kb * jnp.exp(gcum)[..., None]], axis=-1)
    sol = lax.linalg.triangular_solve(a_mat + jnp.eye(c, dtype=a_mat.dtype), rhs,
                                      left_side=True, lower=True, unit_diagonal=True)
    u0, w = sol[..., :dv], sol[..., dv:]
    qk = jnp.einsum('bnhtd,bnhsd->bnhts', qh, kh) * decay
    q_dec = qh * jnp.exp(gcum)[..., None]
    g_last = gcum[..., -1]
    k_dec = kh * jnp.exp(g_last[..., None] - gcum)[..., None]

    def step(s, xs):
        u0c, wc, qkc, qdc, kdc, glc = xs
        u = u0c - jnp.einsum('bhtk,bhkv->bhtv', wc, s)
        o = jnp.einsum('bhtk,bhkv->bhtv', qdc, s) + jnp.einsum('bhts,bhsv->bhtv', qkc, u)
        s = jnp.exp(glc)[..., None, None] * s + jnp.einsum('bhsk,bhsv->bhkv', kdc, u)
        return s, o

    xs = tuple(jnp.moveaxis(a, 1, 0) for a in (u0, w, qk, q_dec, k_dec, g_last))
    s_fin, o = lax.scan(step, s0, xs)
    o = jnp.transpose(o, (1, 0, 3, 2, 4)).reshape(b, t, h, dv)
    return o, s_fin


def _hgrn2_mixer(q_raw, f_raw, i_raw, g_raw, lb, norm_w, s0):
    b, t, _ = q_raw.shape
    k = (1.0 - lb) * jax.nn.sigmoid(-f_raw)
    log_f = jnp.log1p(-jnp.minimum(k, 1.0 - 1e-6))
    q = jax.nn.silu(q_raw)
    shp = (b, t, B_HEADS, B_DK)
    o, s_fin = _gla_chunked(q.reshape(shp), k.reshape(shp), log_f.reshape(shp),
                            i_raw.reshape(b, t, B_HEADS, B_DV), s0)
    o = _rms_f32(o, norm_w) * jax.nn.silu(g_raw.reshape(b, t, B_HEADS, B_DV))
    return o.reshape(b, t, B_HEADS * B_DV), s_fin


def _even_mixer(proj_a, proj_b, bsz, t, past, aqn, akn, lb, bnorm):
    pa = proj_a.reshape(bsz, t, -1)
    pb = proj_b.reshape(bsz, t, -1)
    aq, ak, av = pa[..., :2048], pa[..., 2048:2560], pa[..., 2560:3072]
    iq, ik, iw = pa[..., 3072:7168], pa[..., 7168:7296], pa[..., 7296:7328]
    bq, bf, bi, bg = [pb[..., i * 2048:(i + 1) * 2048] for i in range(4)]
    past_len = 0 if past is None else past[0].shape[1]
    q_pos = past_len + jnp.arange(t, dtype=jnp.int32)
    aq = _rope(_rms_f32(aq.reshape(bsz, t, A_HEADS, HEAD_DIM), aqn), q_pos, HEAD_DIM)
    ak = _rope(_rms_f32(ak.reshape(bsz, t, A_KV_HEADS, HEAD_DIM), akn), q_pos, HEAD_DIM)
    av = av.reshape(bsz, t, A_KV_HEADS, HEAD_DIM)
    iq = _rope(iq.reshape(bsz, t, IDX_HEADS, IDX_DIM), q_pos, IDX_ROPE_DIM)
    ik = _rope(ik.reshape(bsz, t, 1, IDX_DIM), q_pos, IDX_ROPE_DIM)[:, :, 0]
    if past is None:
        keys, vals, ikeys = ak, av, ik
        s0_b = jnp.zeros((bsz, B_HEADS, B_DK, B_DV), F32)
    else:
        keys = jnp.concatenate([past[0], ak], axis=1)
        vals = jnp.concatenate([past[1], av], axis=1)
        ikeys = jnp.concatenate([past[2], ik], axis=1)
        s0_b = past[3]
    lkeys = keys.shape[1]
    o_a = dsa_attention_pallas(aq.reshape(bsz, t, -1), keys.reshape(bsz, lkeys, -1),
                               vals.reshape(bsz, lkeys, -1), iq.reshape(bsz, t, -1), ikeys, iw,
                               past_len)
    o_b, s_b = _hgrn2_mixer(bq, bf, bi, bg, lb, bnorm, s0_b)
    mixed_in = jnp.concatenate([o_a.astype(F32), o_b], axis=-1)
    return mixed_in.reshape(bsz * t, -1), (ak, av, ik, s_b)


def _odd_mixer(proj_m, proj_s, bsz, t, past, conv_w, a_log, dt_bias, onorm):
    pm = proj_m.reshape(bsz, t, -1)
    ps = proj_s.reshape(bsz, t, -1)
    qkv, z = pm[..., :C_CONV_CH], pm[..., C_CONV_CH:]
    b_raw, a_raw = ps[..., :C_V_HEADS], ps[..., C_V_HEADS:2 * C_V_HEADS]
    if past is None:
        conv_state = jnp.zeros((bsz, C_CONV - 1, C_CONV_CH), F32)
        s0_c = jnp.zeros((bsz, C_V_HEADS, C_DK, C_DV), F32)
    else:
        s0_c, conv_state = past
    xpad = jnp.concatenate([conv_state, qkv], axis=1)
    conv = sum(xpad[:, tap:tap + t] * conv_w[tap] for tap in range(C_CONV))
    new_conv = xpad[:, -(C_CONV - 1):]
    act = jax.nn.silu(conv)
    nqk = C_QK_HEADS * C_DK
    cq, ck, cv = act[..., :nqk], act[..., nqk:2 * nqk], act[..., 2 * nqk:]
    rep = C_V_HEADS // C_QK_HEADS
    cq = jnp.repeat(_l2norm(cq.reshape(bsz, t, C_QK_HEADS, C_DK)), rep, axis=2) * (C_DK ** -0.5)
    ck = jnp.repeat(_l2norm(ck.reshape(bsz, t, C_QK_HEADS, C_DK)), rep, axis=2)
    cv = cv.reshape(bsz, t, C_V_HEADS, C_DV)
    beta = jax.nn.sigmoid(b_raw)
    g = -jnp.exp(a_log.astype(F32)) * jax.nn.softplus(a_raw + dt_bias.astype(F32))
    o_c, s_c = _gated_delta_chunked(cq, ck, cv, g, beta, s0_c)
    o_c = _rms_f32(o_c, onorm) * jax.nn.silu(z.reshape(bsz, t, C_V_HEADS, C_DV))
    return o_c.reshape(bsz * t, -1), (s_c, new_conv)


def _pad_cols(w, n):
    return jnp.pad(w, ((0, 0), (0, n - w.shape[1])))


def kernel(x_prompt, x_sample, cache_a_k, cache_a_v, cache_a_kidx, state_b, state_c, state_c_conv, p_prompt, p_sample, ffn1_norm, ffn1_w_gate, ffn1_w_up, ffn1_w_down, mix_norm, even_w_in, even_w_out, a_q_norm, a_k_norm, b_lb_logits, b_out_norm, odd_w_in, odd_w_out, c_conv_w, c_a_log, c_dt_bias, c_out_norm, ffn2_norm, ffn2_w_gate, ffn2_w_up, ffn2_w_down, ple_norm, ple_w_gate, ple_w_proj, ple_post_norm):
    d = D_MODEL
    bp, tp = x_prompt.shape[:2]
    bs, ts = x_sample.shape[:2]
    mp, ms = bp * tp, bs * ts
    x = jnp.concatenate([x_prompt.reshape(mp, d), x_sample.reshape(ms, d)], axis=0)
    p_all = jnp.concatenate([p_prompt.reshape(DEPTH, mp, -1), p_sample.reshape(DEPTH, ms, -1)],
                            axis=1).astype(BF16)

    lb_soft = jax.nn.softmax(b_lb_logits.astype(F32), axis=0)
    lbs = jnp.cumsum(lb_soft, axis=0) - lb_soft[0]

    new_p = [[] for _ in range(6)]
    new_s = [[] for _ in range(6)]
    for layer in range(DEPTH):
        j = layer // 2
        x = ffn(x, ffn1_norm[layer], ffn1_w_gate[layer].astype(BF16),
                ffn1_w_up[layer].astype(BF16), ffn1_w_down[layer].astype(BF16))
        h = rmsnorm(x, mix_norm[layer])
        if layer % 2 == 0:
            w_in = even_w_in[j]
            w_a = _pad_cols(w_in[:, :7328], 7680).astype(BF16)
            w_b = w_in[:, 7328:].astype(BF16)
            proj_a = matmul(h, w_a, name="even_in_a")
            proj_b = matmul(h, w_b, name="even_in_b")
            mix_p, st_p = _even_mixer(proj_a[:mp], proj_b[:mp], bp, tp, None,
                                      a_q_norm[j], a_k_norm[j], lbs[j], b_out_norm[j])
            mix_s, st_s = _even_mixer(proj_a[mp:], proj_b[mp:], bs, ts,
                                      (cache_a_k[j], cache_a_v[j], cache_a_kidx[j], state_b[j]),
                                      a_q_norm[j], a_k_norm[j], lbs[j], b_out_norm[j])
            for idx in range(4):
                new_p[idx].append(st_p[idx])
                new_s[idx].append(st_s[idx])
            w_out = even_w_out[j].astype(BF16)
        else:
            w_in = odd_w_in[j]
            w_m = w_in[:, :12288].astype(BF16)
            w_s = _pad_cols(w_in[:, 12288:], 128).astype(BF16)
            proj_m = matmul(h, w_m, name="odd_in_m")
            proj_s = matmul(h, w_s, tn=128, name="odd_in_s")
            mix_p, st_p = _odd_mixer(proj_m[:mp], proj_s[:mp], bp, tp, None,
                                     c_conv_w[j], c_a_log[j], c_dt_bias[j], c_out_norm[j])
            mix_s, st_s = _odd_mixer(proj_m[mp:], proj_s[mp:], bs, ts,
                                     (state_c[j], state_c_conv[j]),
                                     c_conv_w[j], c_a_log[j], c_dt_bias[j], c_out_norm[j])
            for idx in range(2):
                new_p[4 + idx].append(st_p[idx])
                new_s[4 + idx].append(st_s[idx])
            w_out = odd_w_out[j].astype(BF16)
        mix_in = jnp.concatenate([mix_p, mix_s], axis=0).astype(BF16)
        x = matmul(mix_in, w_out, x, body=_mm_resid_body, name="mix_out")
        x = ffn(x, ffn2_norm[layer], ffn2_w_gate[layer].astype(BF16),
                ffn2_w_up[layer].astype(BF16), ffn2_w_down[layer].astype(BF16))
        hp = rmsnorm(x, ple_norm[layer])
        pp = ple_proj(p_all[layer], ple_w_proj[layer].astype(BF16), ple_post_norm[layer])
        x = matmul(hp, ple_w_gate[layer].astype(BF16), x, pp, body=_mm_gate_body, name="ple_gate")

    y_prompt = x[:mp].reshape(bp, tp, d)
    y_sample = x[mp:].reshape(bs, ts, d)
    outs_p = tuple(jnp.stack(v) for v in new_p)
    outs_s = tuple(jnp.stack(v) for v in new_s)
    return (y_prompt, y_sample) + outs_p + outs_s
```

```python
from functools import partial

import numpy as np

import jax
import jax.numpy as jnp
from jax import lax
from jax.experimental import pallas as pl
from jax.experimental.pallas import tpu as pltpu

D_MODEL = 4096
DEPTH = 4
CHUNK = 64
QBLOCK = 128
EPS = 1e-6
NEG_BIG = -1e30
ROPE_THETA = 10000.0
HEAD_DIM = 128
A_HEADS = 16
A_KV_HEADS = 4
IDX_HEADS = 32
IDX_DIM = 128
IDX_ROPE_DIM = 64
TOPK_MAX = 256
IDX_SCALE = (IDX_HEADS * IDX_DIM) ** -0.5
B_HEADS = 16
B_DK = 128
B_DV = 128
C_QK_HEADS = 16
C_V_HEADS = 32
C_DK = 128
C_DV = 128
C_CONV = 4
C_CONV_CH = 2 * C_QK_HEADS * C_DK + C_V_HEADS * C_DV

V7X_VMEM_LIMIT_BYTES = 56 * 1024 * 1024

BF16 = jnp.bfloat16
F32 = jnp.float32


def _params(*sem):
    return pltpu.CompilerParams(dimension_semantics=sem,
                                vmem_limit_bytes=V7X_VMEM_LIMIT_BYTES)


def _rmsnorm_body(x_ref, w_ref, o_ref):
    x = x_ref[...]
    ms = jnp.mean(x * x, axis=-1, keepdims=True)
    o_ref[...] = (x * lax.rsqrt(ms + EPS) * w_ref[...]).astype(o_ref.dtype)


def rmsnorm(x, w, tm=512, out_dtype=BF16):
    m, d = x.shape
    return pl.pallas_call(
        _rmsnorm_body,
        grid=(m // tm,),
        in_specs=[pl.BlockSpec((tm, d), lambda i: (i, 0)),
                  pl.BlockSpec((1, d), lambda i: (0, 0))],
        out_specs=pl.BlockSpec((tm, d), lambda i: (i, 0)),
        out_shape=jax.ShapeDtypeStruct((m, d), out_dtype),
        compiler_params=_params("parallel"),
        name="rmsnorm",
    )(x, w.reshape(1, d).astype(F32))


def _mm_body(a_ref, w_ref, o_ref):
    o_ref[...] = jnp.dot(a_ref[...], w_ref[...], preferred_element_type=F32)


def _mm_resid_body(a_ref, w_ref, r_ref, o_ref):
    o_ref[...] = r_ref[...] + jnp.dot(a_ref[...], w_ref[...], preferred_element_type=F32)


def _mm_gate_body(a_ref, w_ref, r_ref, p_ref, o_ref):
    acc = jnp.dot(a_ref[...], w_ref[...], preferred_element_type=F32)
    o_ref[...] = r_ref[...] + jax.nn.sigmoid(acc) * p_ref[...]


def matmul(a, w, *extras, tm=1024, tn=512, body=_mm_body, name="matmul"):
    m, k = a.shape
    n = w.shape[1]
    tile = pl.BlockSpec((tm, tn), lambda i, j: (i, j))
    return pl.pallas_call(
        body,
        grid=(m // tm, n // tn),
        in_specs=[pl.BlockSpec((tm, k), lambda i, j: (i, 0)),
                  pl.BlockSpec((k, tn), lambda i, j: (0, j))] + [tile] * len(extras),
        out_specs=tile,
        out_shape=jax.ShapeDtypeStruct((m, n), F32),
        compiler_params=_params("parallel", "arbitrary"),
        name=name,
    )(a, w, *extras)


def _ple_proj_body(p_ref, w_ref, nw_ref, o_ref):
    y = jnp.dot(p_ref[...], w_ref[...], preferred_element_type=F32)
    ms = jnp.mean(y * y, axis=-1, keepdims=True)
    o_ref[...] = y * lax.rsqrt(ms + EPS) * nw_ref[...]


def ple_proj(p, w, nw, tm=512):
    m, k = p.shape
    n = w.shape[1]
    return pl.pallas_call(
        _ple_proj_body,
        grid=(m // tm,),
        in_specs=[pl.BlockSpec((tm, k), lambda i: (i, 0)),
                  pl.BlockSpec((k, n), lambda i: (0, 0)),
                  pl.BlockSpec((1, n), lambda i: (0, 0))],
        out_specs=pl.BlockSpec((tm, n), lambda i: (i, 0)),
        out_shape=jax.ShapeDtypeStruct((m, n), F32),
        compiler_params=_params("parallel"),
        name="ple_proj",
    )(p, w, nw.reshape(1, n).astype(F32))


def _ffn_body(x_ref, nw_ref, wg_ref, wu_ref, wd_ref, o_ref, h_ref):
    f = pl.program_id(1)

    @pl.when(f == 0)
    def _():
        x = x_ref[...]
        ms = jnp.mean(x * x, axis=-1, keepdims=True)
        h_ref[...] = (x * lax.rsqrt(ms + EPS) * nw_ref[...]).astype(BF16)

        o_ref[...] = jnp.zeros_like(o_ref)

    h = h_ref[...]
    g = jnp.dot(h, wg_ref[...], preferred_element_type=F32)
    u = jnp.dot(h, wu_ref[...], preferred_element_type=F32)
    act = (g * jax.nn.sigmoid(g) * u).astype(BF16)
    o_ref[...] += jnp.dot(act, wd_ref[...], preferred_element_type=F32)

    @pl.when(f == pl.num_programs(1) - 1)
    def _():
        o_ref[...] = x_ref[...] + 0.5 * o_ref[...]


def ffn(x, nw, wg, wu, wd, tm=512, tf=256):
    m, d = x.shape
    dff = wg.shape[1]
    return pl.pallas_call(
        _ffn_body,
        grid=(m // tm, dff // tf),
        in_specs=[pl.BlockSpec((tm, d), lambda i, f: (i, 0), pipeline_mode=pl.Buffered(1)),
                  pl.BlockSpec((1, d), lambda i, f: (0, 0)),
                  pl.BlockSpec((d, tf), lambda i, f: (0, f)),
                  pl.BlockSpec((d, tf), lambda i, f: (0, f)),
                  pl.BlockSpec((tf, d), lambda i, f: (f, 0))],
        out_specs=pl.BlockSpec((tm, d), lambda i, f: (i, 0)),
        out_shape=jax.ShapeDtypeStruct((m, d), F32),
        scratch_shapes=[pltpu.VMEM((tm, d), BF16)],
        compiler_params=_params("parallel", "arbitrary"),
        name="ffn",
    )(x, nw.reshape(1, d).astype(F32), wg, wu, wd)


DSA_KEY_TILE = 256
DSA_HEAD_GROUP = 8
INT32_MIN = -2 ** 31
A_GROUP = A_HEADS // A_KV_HEADS
NT_DIMS = (((1,), (1,)), ((), ()))


def _dsa_body(qi_ref, wi_ref, q_ref, ki_ref, k_ref, v_ref, o_ref,
              qi_st, wib, skey, q_st, m_s, l_s, acc_s, *, qb, past_len, l_valid, topk):
    lk = DSA_KEY_TILE
    i = pl.program_id(1)
    q_pos0 = past_len + i * qb
    limit = jnp.minimum(((q_pos0 + qb - 1) // CHUNK + 1) * CHUNK, l_valid)
    nt = (limit + lk - 1) // lk

    for h in range(IDX_HEADS):
        qi_st[h * qb:(h + 1) * qb, :] = qi_ref[0, :, h * IDX_DIM:(h + 1) * IDX_DIM]
        wib[h] = jnp.broadcast_to(wi_ref[0, :, h:h + 1], (qb, 128))
    for n in range(A_KV_HEADS):
        for g in range(A_GROUP):
            hd = n * A_GROUP + g
            q_st[n, g * qb:(g + 1) * qb, :] = q_ref[0, :, hd * HEAD_DIM:(hd + 1) * HEAD_DIM]

    q_chunk = (q_pos0 + lax.broadcasted_iota(jnp.int32, (qb, 128), 0)) // CHUNK

    def score_tile(t, carry):
        kt = ki_ref[0, t]
        cols = [jnp.zeros((qb, 128), F32) for _ in range(lk // 128)]
        for hg in range(IDX_HEADS // DSA_HEAD_GROUP):
            rows = DSA_HEAD_GROUP * qb
            lg = lax.dot_general(qi_st[hg * rows:(hg + 1) * rows, :], kt, NT_DIMS,
                                 preferred_element_type=F32)
            for hh in range(DSA_HEAD_GROUP):
                w = wib[hg * DSA_HEAD_GROUP + hh]
                for c in range(lk // 128):
                    blk = lg[hh * qb:(hh + 1) * qb, c * 128:(c + 1) * 128]
                    cols[c] = cols[c] + jnp.maximum(blk, 0.0) * w
        for c in range(lk // 128):
            k_pos = t * lk + c * 128 + lax.broadcasted_iota(jnp.int32, (qb, 128), 1)
            adm = (k_pos // CHUNK <= q_chunk) & (k_pos < l_valid)
            bits = lax.bitcast_convert_type(cols[c], jnp.int32)
            key = jnp.where(bits < 0, bits ^ jnp.int32(0x7FFFFFFF), bits)
            skey[t, :, c * 128:(c + 1) * 128] = jnp.where(adm, key, jnp.int32(INT32_MIN))
        return carry

    lax.fori_loop(0, nt, score_tile, 0)

    def count_ge(cand):
        def body(t, acc):
            for c in range(lk // 128):
                acc = acc + jnp.where(skey[t, :, c * 128:(c + 1) * 128] >= cand, 1.0, 0.0)
            return acc
        acc = lax.fori_loop(0, nt, body, jnp.zeros((qb, 128), F32))
        return jnp.sum(acc, axis=1, keepdims=True)

    kf = jnp.float32(topk)
    zero = jnp.zeros((qb, 128), jnp.int32)
    prefix = jnp.where(count_ge(zero) >= kf, zero, jnp.int32(INT32_MIN))

    def bit_step(s, prefix):
        cand = prefix + lax.shift_left(jnp.int32(1), jnp.int32(30) - s)
        return jnp.where(count_ge(cand) >= kf, cand, prefix)

    prefix = lax.fori_loop(0, 31, bit_step, prefix)
    thr = jnp.maximum(prefix, jnp.int32(INT32_MIN + 1))

    m_s[...] = jnp.full(m_s.shape, NEG_BIG, F32)
    l_s[...] = jnp.zeros(l_s.shape, F32)
    acc_s[...] = jnp.zeros(acc_s.shape, F32)

    def attn_tile(t, carry):
        bias = jnp.concatenate(
            [jnp.where(skey[t, :, c * 128:(c + 1) * 128] >= thr, 0.0, NEG_BIG)
             for c in range(lk // 128)], axis=1)
        bias4 = jnp.concatenate([bias] * A_GROUP, axis=0)
        for n in range(A_KV_HEADS):
            kt = k_ref[0, t, :, n * HEAD_DIM:(n + 1) * HEAD_DIM]
            vt = v_ref[0, t, :, n * HEAD_DIM:(n + 1) * HEAD_DIM]
            s = lax.dot_general(q_st[n], kt, NT_DIMS, preferred_element_type=F32) + bias4
            m_old = m_s[n]
            m_new = jnp.maximum(m_old, jnp.max(s, axis=1, keepdims=True))
            alpha = jnp.exp(m_old - m_new)
            p = jnp.exp(s - m_new)
            l_s[n] = alpha * l_s[n] + jnp.sum(p, axis=1, keepdims=True)
            acc_s[n] = alpha * acc_s[n] + jnp.dot(p.astype(BF16), vt, preferred_element_type=F32)
            m_s[n] = m_new
        return carry

    lax.fori_loop(0, nt, attn_tile, 0)

    for n in range(A_KV_HEADS):
        o = acc_s[n] / l_s[n]
        for g in range(A_GROUP):
            hd = n * A_GROUP + g
            o_ref[0, :, hd * HEAD_DIM:(hd + 1) * HEAD_DIM] = o[g * qb:(g + 1) * qb, :].astype(o_ref.dtype)


def dsa_attention_pallas(q, k, v, qi, ki, wi, past_len):
    bsz, t, _ = q.shape
    l = k.shape[1]
    lk = DSA_KEY_TILE
    topk = min(TOPK_MAX, l // 4)
    qb = min(QBLOCK, t)
    n_tiles = -(-l // lk)
    pad = n_tiles * lk - l

    def tiles(a):
        a = jnp.pad(a, ((0, 0), (0, pad), (0, 0))).astype(BF16)
        return a.reshape(bsz, n_tiles, lk, a.shape[-1])

    qs = (q * (HEAD_DIM ** -0.5)).astype(BF16)
    whole = lambda shape: pl.BlockSpec(shape, lambda b, i: (b, 0, 0, 0), pipeline_mode=pl.Buffered(1))
    rows = lambda width: pl.BlockSpec((1, qb, width), lambda b, i: (b, i, 0))
    body = partial(_dsa_body, qb=qb, past_len=past_len, l_valid=l, topk=topk)
    return pl.pallas_call(
        body,
        grid=(bsz, t // qb),
        in_specs=[rows(IDX_HEADS * IDX_DIM), rows(IDX_HEADS), rows(A_HEADS * HEAD_DIM),
                  whole((1, n_tiles, lk, IDX_DIM)),
                  whole((1, n_tiles, lk, A_KV_HEADS * HEAD_DIM)),
                  whole((1, n_tiles, lk, A_KV_HEADS * HEAD_DIM))],
        out_specs=rows(A_HEADS * HEAD_DIM),
        out_shape=jax.ShapeDtypeStruct((bsz, t, A_HEADS * HEAD_DIM), BF16),
        scratch_shapes=[pltpu.VMEM((IDX_HEADS * qb, IDX_DIM), BF16),
                        pltpu.VMEM((IDX_HEADS, qb, 128), F32),
                        pltpu.VMEM((n_tiles, qb, lk), jnp.int32),
                        pltpu.VMEM((A_KV_HEADS, A_GROUP * qb, HEAD_DIM), BF16),
                        pltpu.VMEM((A_KV_HEADS, A_GROUP * qb, 1), F32),
                        pltpu.VMEM((A_KV_HEADS, A_GROUP * qb, 1), F32),
                        pltpu.VMEM((A_KV_HEADS, A_GROUP * qb, HEAD_DIM), F32)],
        compiler_params=_params("parallel", "arbitrary"),
        name="dsa_attention",
    )(qi.astype(BF16), wi * IDX_SCALE, qs, tiles(ki), tiles(k), tiles(v))


GLA_LEVELS = 6
GLA_ROWS_PER_STEP = 512


def _gla_exponent_matrix():
    c = CHUNK
    a = np.zeros((GLA_LEVELS + 1, c, c), np.float32)
    a[0] = np.tril(np.ones((c, c), np.float32))
    for l in range(GLA_LEVELS):
        b = (c // 2) >> l
        for t in range(c):
            ref = (t // (2 * b)) * 2 * b + b
            if t >= ref:
                a[l + 1, t, ref + 1:t + 1] = 1.0
            else:
                a[l + 1, t, t + 1:ref + 1] = 1.0
    return a.reshape((GLA_LEVELS + 1) * c, c)


def _split3_dot(a_bf16, x):
    hi = x.astype(BF16)
    r1 = x - hi.astype(F32)
    mid = r1.astype(BF16)
    lo = (r1 - mid.astype(F32)).astype(BF16)
    return (jnp.dot(a_bf16, hi, preferred_element_type=F32)
            + jnp.dot(a_bf16, mid, preferred_element_type=F32)
            + jnp.dot(a_bf16, lo, preferred_element_type=F32))


TN_DIMS = (((0,), (0,)), ((), ()))


def _gla_body(q_ref, f_ref, i_ref, g_ref, lb_ref, nw_ref, a_ref, s0_ref, o_ref, sfin_ref, st_ref,
              *, n_chunks):
    r = pl.program_id(2)
    c = CHUNK

    @pl.when(r == 0)
    def _():
        st_ref[...] = s0_ref[0, 0].T

    lb = lb_ref[...]
    nw = nw_ref[...]
    amat = a_ref[...]
    t_row = lax.broadcasted_iota(jnp.int32, (c, B_DK), 0)
    t_idx = lax.broadcasted_iota(jnp.int32, (c, c), 0)
    s_idx = lax.broadcasted_iota(jnp.int32, (c, c), 1)
    upper = [((t_row >> (GLA_LEVELS - 1 - l)) & 1) == 1 for l in range(GLA_LEVELS)]
    pair = [((t_idx >> (GLA_LEVELS - l)) == (s_idx >> (GLA_LEVELS - l)))
            & (((t_idx >> (GLA_LEVELS - 1 - l)) & 1) == 1)
            & (((s_idx >> (GLA_LEVELS - 1 - l)) & 1) == 0) for l in range(GLA_LEVELS)]
    eye = t_idx == s_idx

    def chunk(ci, carry):
        rows = pl.ds(pl.multiple_of(ci * c, c), c)
        fr = f_ref[rows, :]
        qr = q_ref[rows, :]
        v = i_ref[rows, :]
        gr = g_ref[rows, :]
        k = (1.0 - lb) * jax.nn.sigmoid(-fr)
        logf = jnp.log1p(-jnp.minimum(k, 1.0 - 1e-6))
        q = qr * jax.nn.sigmoid(qr)
        d_all = _split3_dot(amat, logf)
        gcum = d_all[0:c]
        att = jnp.where(eye, jnp.sum(q * k, axis=1, keepdims=True), 0.0)
        for l in range(GLA_LEVELS):
            e = jnp.exp(d_all[(l + 1) * c:(l + 2) * c])
            x = (jnp.where(upper[l], q, k) * e).astype(BF16)
            gram = lax.dot_general(x, x, NT_DIMS, preferred_element_type=F32)
            att = att + jnp.where(pair[l], gram, 0.0)
        g_last = gcum[c - 1:c]
        q_dec = (q * jnp.exp(gcum)).astype(BF16)
        k_dec = (k * jnp.exp(g_last - gcum)).astype(BF16)
        vb = v.astype(BF16)
        st = st_ref[...]
        o = (jnp.dot(att.astype(BF16), vb, preferred_element_type=F32)
             + lax.dot_general(q_dec, st.astype(BF16), NT_DIMS, preferred_element_type=F32))
        st_ref[...] = st * jnp.exp(g_last) + lax.dot_general(vb, k_dec, TN_DIMS,
                                                             preferred_element_type=F32)
        o = o * lax.rsqrt(jnp.mean(o * o, axis=1, keepdims=True) + EPS) * nw
        o_ref[rows, :] = (o * (gr * jax.nn.sigmoid(gr))).astype(o_ref.dtype)
        return carry

    lax.fori_loop(0, n_chunks, chunk, 0, unroll=True)

    @pl.when(r == pl.num_programs(2) - 1)
    def _():
        sfin_ref[0, 0] = st_ref[...].T


def hgrn2_pallas(proj_b, row0, bsz, t, lb, norm_w, s0):
    rps = min(t, GLA_ROWS_PER_STEP)
    steps = t // rps
    blk0 = row0 // rps
    seg = lambda s: pl.BlockSpec((rps, B_DK), lambda b, h, r: (blk0 + b * steps + r, s * B_HEADS + h))
    state = pl.BlockSpec((1, 1, B_DK, B_DV), lambda b, h, r: (b, h, 0, 0))
    amat = jnp.asarray(_gla_exponent_matrix(), BF16)
    return pl.pallas_call(
        partial(_gla_body, n_chunks=rps // CHUNK),
        grid=(bsz, B_HEADS, steps),
        in_specs=[seg(0), seg(1), seg(2), seg(3),
                  pl.BlockSpec((1, B_DK), lambda b, h, r: (0, h)),
                  pl.BlockSpec((1, B_DV), lambda b, h, r: (0, 0)),
                  pl.BlockSpec(amat.shape, lambda b, h, r: (0, 0)),
                  state],
        out_specs=[pl.BlockSpec((rps, B_DV), lambda b, h, r: (b * steps + r, h)), state],
        out_shape=[jax.ShapeDtypeStruct((bsz * t, B_HEADS * B_DV), BF16),
                   jax.ShapeDtypeStruct((bsz, B_HEADS, B_DK, B_DV), F32)],
        scratch_shapes=[pltpu.VMEM((B_DV, B_DK), F32)],
        compiler_params=_params("parallel", "parallel", "arbitrary"),
        name="hgrn2",
    )(proj_b, proj_b, proj_b, proj_b, lb.reshape(1, -1).astype(F32),
      norm_w.reshape(1, -1).astype(F32), amat, s0)


GDN_PROBLEMS_PER_STEP = 8
GDN_CHUNKS_PER_STEP = 2


def _split3_dot_tn(x, u_bf16):
    hi = x.astype(BF16)
    r1 = x - hi.astype(F32)
    mid = r1.astype(BF16)
    lo = (r1 - mid.astype(F32)).astype(BF16)
    return (lax.dot_general(hi, u_bf16, TN_DIMS, preferred_element_type=F32)
            + lax.dot_general(mid, u_bf16, TN_DIMS, preferred_element_type=F32)
            + lax.dot_general(lo, u_bf16, TN_DIMS, preferred_element_type=F32))


def _gdn_body(q_ref, k_ref, v_ref, z_ref, beta_ref, g_ref, nw_ref, s0_ref, o_ref, sfin_ref, s_ref,
              *, n_chunks, heads):
    hg = pl.program_id(1)
    r = pl.program_id(2)
    c = CHUNK
    rep = C_V_HEADS // C_QK_HEADS

    @pl.when(r == 0)
    def _():
        s_ref[...] = s0_ref[0]

    nw = nw_ref[...]
    t_idx = lax.broadcasted_iota(jnp.int32, (c, c), 0)
    s_idx = lax.broadcasted_iota(jnp.int32, (c, c), 1)
    incl = s_idx <= t_idx
    strict = s_idx < t_idx
    eye = jnp.where(t_idx == s_idx, 1.0, 0.0)
    tril_b = jnp.where(incl, 1.0, 0.0).astype(BF16)
    triu_b = jnp.where(t_idx <= s_idx, 1.0, 0.0).astype(BF16)
    pair = [((t_idx >> (l + 1)) == (s_idx >> (l + 1)))
            & (((t_idx >> l) & 1) == 1) & (((s_idx >> l) & 1) == 0) for l in range(GLA_LEVELS)]
    lane_head = lax.broadcasted_iota(jnp.int32, (c, C_V_HEADS), 1)

    probs = [(j, ci) for j in range(heads) for ci in range(n_chunks)]
    cs = range(len(probs))
    rows = [slice(ci * c, (ci + 1) * c) for _, ci in probs]
    vcol = [slice(j * C_DV, (j + 1) * C_DV) for j, _ in probs]
    qcol = [slice((j // rep) * C_DK, (j // rep + 1) * C_DK) for j, _ in probs]
    head_sel = [lane_head == hg * heads + j for j, _ in probs]
    qa = [q_ref[rw, qc] for rw, qc in zip(rows, qcol)]
    ka = [k_ref[rw, qc] for rw, qc in zip(rows, qcol)]
    q = [x * lax.rsqrt(jnp.sum(x * x, axis=1, keepdims=True) + EPS) * (C_DK ** -0.5) for x in qa]
    k = [x * lax.rsqrt(jnp.sum(x * x, axis=1, keepdims=True) + EPS) for x in ka]
    beta = [jnp.sum(jnp.where(hs, beta_ref[rw, :], 0.0), axis=1, keepdims=True)
            for rw, hs in zip(rows, head_sel)]
    gb = [jnp.broadcast_to(jnp.sum(jnp.where(hs, g_ref[rw, :], 0.0), axis=1, keepdims=True),
                           (c, C_DK)) for rw, hs in zip(rows, head_sel)]
    gcum = [_split3_dot(tril_b, x) for x in gb]
    gc_row = [_split3_dot_tn(x[:, :c], triu_b) for x in gb]
    decay = [jnp.where(incl, jnp.exp(jnp.where(incl, a[:, :c] - b, 0.0)), 0.0)
             for a, b in zip(gcum, gc_row)]
    kb = [a * b for a, b in zip(k, beta)]
    kbf = [x.astype(BF16) for x in k]
    a_mat = [jnp.where(strict, lax.dot_general(a.astype(BF16), b, NT_DIMS,
                                               preferred_element_type=F32) * d, 0.0)
             for a, b, d in zip(kb, kbf, decay)]
    tinv = [eye for _ in cs]
    for l in range(GLA_LEVELS):
        tb = [x.astype(BF16) for x in tinv]
        xt = [jnp.dot(jnp.where(pair[l], a, 0.0).astype(BF16), b, preferred_element_type=F32)
              for a, b in zip(a_mat, tb)]
        tinv = [t0 - jnp.dot(b, x.astype(BF16), preferred_element_type=F32)
                for t0, b, x in zip(tinv, tb, xt)]
    e_cum = [jnp.exp(x) for x in gcum]
    rhs = [jnp.concatenate([v_ref[rw, vc] * bt, b * e], axis=1)
           for rw, vc, bt, b, e in zip(rows, vcol, beta, kb, e_cum)]
    sol = [x + jnp.dot((t0 - eye).astype(BF16), x.astype(BF16), preferred_element_type=F32)
           for x, t0 in zip(rhs, tinv)]
    qk = [jnp.where(incl, lax.dot_general(a.astype(BF16), b, NT_DIMS,
                                          preferred_element_type=F32) * d, 0.0).astype(BF16)
          for a, b, d in zip(q, kbf, decay)]
    g_last = [x[c - 1:c] for x in gcum]
    q_dec = [(a * e).astype(BF16) for a, e in zip(q, e_cum)]
    k_dec = [(a * jnp.exp(gl - gc)).astype(BF16) for a, gl, gc in zip(k, g_last, gcum)]

    s = [s_ref[j] for j in range(heads)]
    for ci in range(n_chunks):
        ps = [j * n_chunks + ci for j in range(heads)]
        sb = [x.astype(BF16) for x in s]
        ub = [(sol[p][:, :C_DV] - jnp.dot(sol[p][:, C_DV:].astype(BF16), b,
                                          preferred_element_type=F32)).astype(BF16)
              for p, b in zip(ps, sb)]
        o = [jnp.dot(q_dec[p], b, preferred_element_type=F32)
             + jnp.dot(qk[p], u, preferred_element_type=F32) for p, b, u in zip(ps, sb, ub)]
        s = [jnp.exp(g_last[p]) * x + lax.dot_general(k_dec[p], u, TN_DIMS, preferred_element_type=F32)
             for p, x, u in zip(ps, s, ub)]
        for p, x in zip(ps, o):
            x = x * lax.rsqrt(jnp.mean(x * x, axis=1, keepdims=True) + EPS) * nw
            z = z_ref[rows[p], vcol[p]]
            o_ref[rows[p], vcol[p]] = (x * (z * jax.nn.sigmoid(z))).astype(o_ref.dtype)
    for j in range(heads):
        s_ref[j] = s[j]

    @pl.when(r == pl.num_programs(2) - 1)
    def _():
        for j in range(heads):
            sfin_ref[0, j] = s[j]


def gdn_pallas(act, proj_m, beta, g, row0, bsz, t, norm_w, s0):
    n_chunks = min(t // CHUNK, GDN_CHUNKS_PER_STEP)
    heads = GDN_PROBLEMS_PER_STEP // n_chunks
    rps = n_chunks * CHUNK
    steps = t // rps
    blk0 = row0 // rps
    rep = C_V_HEADS // C_QK_HEADS
    qk_w = (heads // rep) * C_DK
    v_w = heads * C_DV
    qk_blocks = C_QK_HEADS * C_DK // qk_w
    qk_spec = lambda seg: pl.BlockSpec((rps, qk_w), lambda b, h, r: (b * steps + r, seg * qk_blocks + h))
    small = pl.BlockSpec((rps, C_V_HEADS), lambda b, h, r: (b * steps + r, 0))
    state = pl.BlockSpec((1, heads, C_DK, C_DV), lambda b, h, r: (b, h, 0, 0))
    return pl.pallas_call(
        partial(_gdn_body, n_chunks=n_chunks, heads=heads),
        grid=(bsz, C_V_HEADS // heads, steps),
        in_specs=[qk_spec(0), qk_spec(1),
                  pl.BlockSpec((rps, v_w), lambda b, h, r: (b * steps + r, 2 * C_QK_HEADS * C_DK // v_w + h)),
                  pl.BlockSpec((rps, v_w), lambda b, h, r: (blk0 + b * steps + r, C_CONV_CH // v_w + h)),
                  small, small,
                  pl.BlockSpec((1, C_DV), lambda b, h, r: (0, 0)),
                  state],
        out_specs=[pl.BlockSpec((rps, v_w), lambda b, h, r: (b * steps + r, h)), state],
        out_shape=[jax.ShapeDtypeStruct((bsz * t, C_V_HEADS * C_DV), BF16),
                   jax.ShapeDtypeStruct((bsz, C_V_HEADS, C_DK, C_DV), F32)],
        scratch_shapes=[pltpu.VMEM((heads, C_DK, C_DV), F32)],
        compiler_params=_params("parallel", "parallel", "arbitrary"),
        name="gated_deltanet",
    )(act, act, act, proj_m, beta, g, norm_w.reshape(1, -1).astype(F32), s0)


def _rms_f32(x, w):
    y = x * lax.rsqrt(jnp.mean(x * x, axis=-1, keepdims=True) + EPS)
    return y * w.astype(F32)


def _l2norm(x):
    return x * lax.rsqrt(jnp.sum(x * x, axis=-1, keepdims=True) + EPS)


def _rope(x, pos, rot_dim):
    half = rot_dim // 2
    inv_freq = ROPE_THETA ** (-jnp.arange(half, dtype=F32) / half)
    ang = pos.astype(F32)[:, None] * inv_freq[None, :]
    cos = jnp.cos(ang)[None, :, None, :]
    sin = jnp.sin(ang)[None, :, None, :]
    x1, x2, rest = x[..., :half], x[..., half:rot_dim], x[..., rot_dim:]
    return jnp.concatenate([x1 * cos - x2 * sin, x2 * cos + x1 * sin, rest], axis=-1)


def _dsa_attention(q, k, v, qi, ki, wi, q_pos, k_pos):
    b, t = q.shape[:2]
    l = k.shape[1]
    topk = min(TOPK_MAX, l // 4)
    qb = min(QBLOCK, t)
    nb = t // qb
    grp = A_HEADS // A_KV_HEADS
    k_chunk = k_pos // CHUNK
    scale = HEAD_DIM ** -0.5

    def blocks(a):
        return jnp.moveaxis(a.reshape((b, nb, qb) + a.shape[2:]), 1, 0)

    def one_block(args):
        q_b, qi_b, wi_b, pos_b = args
        q_chunk = pos_b // CHUNK
        logits = jnp.einsum('bqhd,bkd->bqhk', qi_b, ki).astype(F32)
        score = jnp.einsum('bqh,bqhk->bqk', wi_b, jax.nn.relu(logits)) * IDX_SCALE
        admissible = k_chunk[None, :] <= q_chunk[:, None]
        score = jnp.where(admissible[None], score, NEG_BIG)
        _, top_idx = lax.top_k(score, topk)
        valid = k_chunk[top_idx] <= q_chunk[None, :, None]
        kg = jax.vmap(lambda kk, ii: kk[ii])(k, top_idx)
        vg = jax.vmap(lambda vv, ii: vv[ii])(v, top_idx)
        qg = q_b.reshape(b, qb, A_KV_HEADS, grp, HEAD_DIM)
        s = jnp.einsum('bqngd,bqknd->bqngk', qg, kg).astype(F32) * scale
        s = jnp.where(valid[:, :, None, None, :], s, NEG_BIG)
        p = jax.nn.softmax(s, axis=-1)
        o = jnp.einsum('bqngk,bqknd->bqngd', p, vg)
        return o.reshape(b, qb, A_HEADS, HEAD_DIM)

    out = lax.map(one_block, (blocks(q), blocks(qi), blocks(wi), q_pos.reshape(nb, qb)))
    return jnp.moveaxis(out, 0, 1).reshape(b, t, A_HEADS, HEAD_DIM)


def _gla_chunked(q, k, g, v, s0):
    b, t, h, dk = q.shape
    dv = v.shape[-1]
    c = min(CHUNK, t)
    n = t // c
    causal = jnp.tril(jnp.ones((c, c), dtype=bool))[None, :, :, None, None]

    def cm(a):
        return jnp.moveaxis(a.reshape((b, n, c) + a.shape[2:]), 1, 0)

    def step(s, xs):
        qc, kc, gc, vc = xs
        gcum = jnp.cumsum(gc, axis=1)
        o_inter = jnp.einsum('bthc,bhcv->bthv', qc * jnp.exp(gcum), s)
        diff = gcum[:, :, None] - gcum[:, None, :]
        w = jnp.where(causal, jnp.exp(jnp.where(causal, diff, 0.0)), 0.0)
        att = jnp.einsum('bthc,bshc,btshc->bths', qc, kc, w)
        o_intra = jnp.einsum('bths,bshv->bthv', att, vc)
        g_last = gcum[:, -1]
        s_new = jnp.exp(g_last)[..., None] * s + jnp.einsum(
            'bshc,bshv->bhcv', kc * jnp.exp(g_last[:, None] - gcum), vc)
        return s_new, o_inter + o_intra

    s_fin, o = lax.scan(step, s0, (cm(q), cm(k), cm(g), cm(v)))
    return jnp.moveaxis(o, 0, 1).reshape(b, t, h, dv), s_fin


def _gated_delta_chunked(q, k, v, g, beta, s0):
    b, t, h, dk = q.shape
    dv = v.shape[-1]
    c = min(CHUNK, t)
    n = t // c

    def heads_first(a):
        a = a.reshape((b, n, c, h) + a.shape[3:])
        return jnp.moveaxis(a, 3, 2)

    qh, kh, vh, gh, bh = [heads_first(a) for a in (q, k, v, g, beta)]
    gcum = jnp.cumsum(gh, axis=-1)
    diff = gcum[..., :, None] - gcum[..., None, :]
    incl = jnp.tril(jnp.ones((c, c), dtype=bool))
    strict = jnp.tril(jnp.ones((c, c), dtype=bool), -1)
    decay = jnp.where(incl, jnp.exp(jnp.where(incl, diff, 0.0)), 0.0)
    kb = kh * bh[..., None]
    a_mat = jnp.where(strict, jnp.einsum('bnhtd,bnhsd->bnhts', kb, kh) * decay, 0.0)
    rhs = jnp.concatenate([vh * bh[..., None], kb * jnp.exp(gcum)[..., None]], axis=-1)
    sol = lax.linalg.triangular_solve(a_mat + jnp.eye(c, dtype=a_mat.dtype), rhs,
                                      left_side=True, lower=True, unit_diagonal=True)
    u0, w = sol[..., :dv], sol[..., dv:]
    qk = jnp.einsum('bnhtd,bnhsd->bnhts', qh, kh) * decay
    q_dec = qh * jnp.exp(gcum)[..., None]
    g_last = gcum[..., -1]
    k_dec = kh * jnp.exp(g_last[..., None] - gcum)[..., None]

    def step(s, xs):
        u0c, wc, qkc, qdc, kdc, glc = xs
        u = u0c - jnp.einsum('bhtk,bhkv->bhtv', wc, s)
        o = jnp.einsum('bhtk,bhkv->bhtv', qdc, s) + jnp.einsum('bhts,bhsv->bhtv', qkc, u)
        s = jnp.exp(glc)[..., None, None] * s + jnp.einsum('bhsk,bhsv->bhkv', kdc, u)
        return s, o

    xs = tuple(jnp.moveaxis(a, 1, 0) for a in (u0, w, qk, q_dec, k_dec, g_last))
    s_fin, o = lax.scan(step, s0, xs)
    o = jnp.transpose(o, (1, 0, 3, 2, 4)).reshape(b, t, h, dv)
    return o, s_fin


def _hgrn2_mixer(q_raw, f_raw, i_raw, g_raw, lb, norm_w, s0):
    b, t, _ = q_raw.shape
    k = (1.0 - lb) * jax.nn.sigmoid(-f_raw)
    log_f = jnp.log1p(-jnp.minimum(k, 1.0 - 1e-6))
    q = jax.nn.silu(q_raw)
    shp = (b, t, B_HEADS, B_DK)
    o, s_fin = _gla_chunked(q.reshape(shp), k.reshape(shp), log_f.reshape(shp),
                            i_raw.reshape(b, t, B_HEADS, B_DV), s0)
    o = _rms_f32(o, norm_w) * jax.nn.silu(g_raw.reshape(b, t, B_HEADS, B_DV))
    return o.reshape(b, t, B_HEADS * B_DV), s_fin


def _even_mixer(proj_a, proj_b, row0, bsz, t, past, aqn, akn, lb, bnorm):
    pa = proj_a.reshape(bsz, t, -1)
    aq, ak, av = pa[..., :2048], pa[..., 2048:2560], pa[..., 2560:3072]
    iq, ik, iw = pa[..., 3072:7168], pa[..., 7168:7296], pa[..., 7296:7328]
    past_len = 0 if past is None else past[0].shape[1]
    q_pos = past_len + jnp.arange(t, dtype=jnp.int32)
    aq = _rope(_rms_f32(aq.reshape(bsz, t, A_HEADS, HEAD_DIM), aqn), q_pos, HEAD_DIM)
    ak = _rope(_rms_f32(ak.reshape(bsz, t, A_KV_HEADS, HEAD_DIM), akn), q_pos, HEAD_DIM)
    av = av.reshape(bsz, t, A_KV_HEADS, HEAD_DIM)
    iq = _rope(iq.reshape(bsz, t, IDX_HEADS, IDX_DIM), q_pos, IDX_ROPE_DIM)
    ik = _rope(ik.reshape(bsz, t, 1, IDX_DIM), q_pos, IDX_ROPE_DIM)[:, :, 0]
    if past is None:
        keys, vals, ikeys = ak, av, ik
        s0_b = jnp.zeros((bsz, B_HEADS, B_DK, B_DV), F32)
    else:
        keys = jnp.concatenate([past[0], ak], axis=1)
        vals = jnp.concatenate([past[1], av], axis=1)
        ikeys = jnp.concatenate([past[2], ik], axis=1)
        s0_b = past[3]
    lkeys = keys.shape[1]
    o_a = dsa_attention_pallas(aq.reshape(bsz, t, -1), keys.reshape(bsz, lkeys, -1),
                               vals.reshape(bsz, lkeys, -1), iq.reshape(bsz, t, -1), ikeys, iw,
                               past_len)
    o_b, s_b = hgrn2_pallas(proj_b, row0, bsz, t, lb, bnorm, s0_b)
    mixed_in = jnp.concatenate([o_a.reshape(bsz * t, -1), o_b], axis=-1)
    return mixed_in, (ak, av, ik, s_b)


def _odd_mixer(proj_m, proj_s, row0, bsz, t, past, conv_w, a_log, dt_bias, onorm):
    m = bsz * t
    qkv = proj_m[row0:row0 + m, :C_CONV_CH].reshape(bsz, t, C_CONV_CH)
    ps = proj_s[row0:row0 + m]
    b_raw, a_raw = ps[:, :C_V_HEADS], ps[:, C_V_HEADS:2 * C_V_HEADS]
    if past is None:
        conv_state = jnp.zeros((bsz, C_CONV - 1, C_CONV_CH), F32)
        s0_c = jnp.zeros((bsz, C_V_HEADS, C_DK, C_DV), F32)
    else:
        s0_c, conv_state = past
    xpad = jnp.concatenate([conv_state, qkv], axis=1)
    conv = sum(xpad[:, tap:tap + t] * conv_w[tap] for tap in range(C_CONV))
    new_conv = xpad[:, -(C_CONV - 1):]
    act = jax.nn.silu(conv).reshape(m, C_CONV_CH)
    beta = jax.nn.sigmoid(b_raw)
    g = -jnp.exp(a_log.astype(F32)) * jax.nn.softplus(a_raw + dt_bias.astype(F32))
    o_c, s_c = gdn_pallas(act, proj_m, beta, g, row0, bsz, t, onorm, s0_c)
    return o_c, (s_c, new_conv)


def _pad_cols(w, n):
    return jnp.pad(w, ((0, 0), (0, n - w.shape[1])))


def kernel(x_prompt, x_sample, cache_a_k, cache_a_v, cache_a_kidx, state_b, state_c, state_c_conv, p_prompt, p_sample, ffn1_norm, ffn1_w_gate, ffn1_w_up, ffn1_w_down, mix_norm, even_w_in, even_w_out, a_q_norm, a_k_norm, b_lb_logits, b_out_norm, odd_w_in, odd_w_out, c_conv_w, c_a_log, c_dt_bias, c_out_norm, ffn2_norm, ffn2_w_gate, ffn2_w_up, ffn2_w_down, ple_norm, ple_w_gate, ple_w_proj, ple_post_norm):
    d = D_MODEL
    bp, tp = x_prompt.shape[:2]
    bs, ts = x_sample.shape[:2]
    mp, ms = bp * tp, bs * ts
    x = jnp.concatenate([x_prompt.reshape(mp, d), x_sample.reshape(ms, d)], axis=0)
    p_all = jnp.concatenate([p_prompt.reshape(DEPTH, mp, -1), p_sample.reshape(DEPTH, ms, -1)],
                            axis=1).astype(BF16)

    lb_soft = jax.nn.softmax(b_lb_logits.astype(F32), axis=0)
    lbs = jnp.cumsum(lb_soft, axis=0) - lb_soft[0]

    new_p = [[] for _ in range(6)]
    new_s = [[] for _ in range(6)]
    for layer in range(DEPTH):
        j = layer // 2
        x = ffn(x, ffn1_norm[layer], ffn1_w_gate[layer].astype(BF16),
                ffn1_w_up[layer].astype(BF16), ffn1_w_down[layer].astype(BF16))
        h = rmsnorm(x, mix_norm[layer])
        if layer % 2 == 0:
            w_in = even_w_in[j]
            w_a = _pad_cols(w_in[:, :7328], 7680).astype(BF16)
            w_b = w_in[:, 7328:].astype(BF16)
            proj_a = matmul(h, w_a, name="even_in_a")
            proj_b = matmul(h, w_b, name="even_in_b")
            mix_p, st_p = _even_mixer(proj_a[:mp], proj_b, 0, bp, tp, None,
                                      a_q_norm[j], a_k_norm[j], lbs[j], b_out_norm[j])
            mix_s, st_s = _even_mixer(proj_a[mp:], proj_b, mp, bs, ts,
                                      (cache_a_k[j], cache_a_v[j], cache_a_kidx[j], state_b[j]),
                                      a_q_norm[j], a_k_norm[j], lbs[j], b_out_norm[j])
            for idx in range(4):
                new_p[idx].append(st_p[idx])
                new_s[idx].append(st_s[idx])
            w_out = even_w_out[j].astype(BF16)
        else:
            w_in = odd_w_in[j]
            w_m = w_in[:, :12288].astype(BF16)
            w_s = _pad_cols(w_in[:, 12288:], 128).astype(BF16)
            proj_m = matmul(h, w_m, name="odd_in_m")
            proj_s = matmul(h, w_s, tn=128, name="odd_in_s")
            mix_p, st_p = _odd_mixer(proj_m, proj_s, 0, bp, tp, None,
                                     c_conv_w[j], c_a_log[j], c_dt_bias[j], c_out_norm[j])
            mix_s, st_s = _odd_mixer(proj_m, proj_s, mp, bs, ts,
                                     (state_c[j], state_c_conv[j]),
                                     c_conv_w[j], c_a_log[j], c_dt_bias[j], c_out_norm[j])
            for idx in range(2):
                new_p[4 + idx].append(st_p[idx])
                new_s[4 + idx].append(st_s[idx])
            w_out = odd_w_out[j].astype(BF16)
        mix_in = jnp.concatenate([mix_p, mix_s], axis=0).astype(BF16)
        x = matmul(mix_in, w_out, x, body=_mm_resid_body, name="mix_out")
        x = ffn(x, ffn2_norm[layer], ffn2_w_gate[layer].astype(BF16),
                ffn2_w_up[layer].astype(BF16), ffn2_w_down[layer].astype(BF16))
        hp = rmsnorm(x, ple_norm[layer])
        pp = ple_proj(p_all[layer], ple_w_proj[layer].astype(BF16), ple_post_norm[layer])
        x = matmul(hp, ple_w_gate[layer].astype(BF16), x, pp, body=_mm_gate_body, name="ple_gate")

    y_prompt = x[:mp].reshape(bp, tp, d)
    y_sample = x[mp:].reshape(bs, ts, d)
    outs_p = tuple(jnp.stack(v) for v in new_p)
    outs_s = tuple(jnp.stack(v) for v in new_s)
    return (y_prompt, y_sample) + outs_p + outs_s
```

```python
from functools import partial

import numpy as np

import jax
import jax.numpy as jnp
from jax import lax
from jax.experimental import pallas as pl
from jax.experimental.pallas import tpu as pltpu

D_MODEL = 4096
DEPTH = 4
CHUNK = 64
QBLOCK = 128
EPS = 1e-6
NEG_BIG = -1e30
ROPE_THETA = 10000.0
HEAD_DIM = 128
A_HEADS = 16
A_KV_HEADS = 4
IDX_HEADS = 32
IDX_DIM = 128
IDX_ROPE_DIM = 64
TOPK_MAX = 256
IDX_SCALE = (IDX_HEADS * IDX_DIM) ** -0.5
B_HEADS = 16
B_DK = 128
B_DV = 128
C_QK_HEADS = 16
C_V_HEADS = 32
C_DK = 128
C_DV = 128
C_CONV = 4
C_CONV_CH = 2 * C_QK_HEADS * C_DK + C_V_HEADS * C_DV

V7X_VMEM_LIMIT_BYTES = 56 * 1024 * 1024

BF16 = jnp.bfloat16
F32 = jnp.float32


def _params(*sem):
    return pltpu.CompilerParams(dimension_semantics=sem,
                                vmem_limit_bytes=V7X_VMEM_LIMIT_BYTES)


def _rmsnorm_body(x_ref, w_ref, o_ref):
    x = x_ref[...]
    ms = jnp.mean(x * x, axis=-1, keepdims=True)
    o_ref[...] = (x * lax.rsqrt(ms + EPS) * w_ref[...]).astype(o_ref.dtype)


def rmsnorm(x, w, tm=512, out_dtype=BF16):
    m, d = x.shape
    return pl.pallas_call(
        _rmsnorm_body,
        grid=(m // tm,),
        in_specs=[pl.BlockSpec((tm, d), lambda i: (i, 0)),
                  pl.BlockSpec((1, d), lambda i: (0, 0))],
        out_specs=pl.BlockSpec((tm, d), lambda i: (i, 0)),
        out_shape=jax.ShapeDtypeStruct((m, d), out_dtype),
        compiler_params=_params("parallel"),
        name="rmsnorm",
    )(x, w.reshape(1, d).astype(F32))


def _mm_body(a_ref, w_ref, o_ref):
    o_ref[...] = jnp.dot(a_ref[...], w_ref[...], preferred_element_type=F32)


def _mm_resid_body(a_ref, w_ref, r_ref, o_ref):
    o_ref[...] = r_ref[...] + jnp.dot(a_ref[...], w_ref[...], preferred_element_type=F32)


def _mm_gate_body(a_ref, w_ref, r_ref, p_ref, o_ref):
    acc = jnp.dot(a_ref[...], w_ref[...], preferred_element_type=F32)
    o_ref[...] = r_ref[...] + jax.nn.sigmoid(acc) * p_ref[...]


def matmul(a, w, *extras, tm=1024, tn=512, body=_mm_body, name="matmul"):
    m, k = a.shape
    n = w.shape[1]
    tile = pl.BlockSpec((tm, tn), lambda i, j: (i, j))
    return pl.pallas_call(
        body,
        grid=(m // tm, n // tn),
        in_specs=[pl.BlockSpec((tm, k), lambda i, j: (i, 0)),
                  pl.BlockSpec((k, tn), lambda i, j: (0, j))] + [tile] * len(extras),
        out_specs=tile,
        out_shape=jax.ShapeDtypeStruct((m, n), F32),
        compiler_params=_params("parallel", "arbitrary"),
        name=name,
    )(a, w, *extras)


def _ple_proj_body(p_ref, w_ref, nw_ref, o_ref):
    y = jnp.dot(p_ref[...], w_ref[...], preferred_element_type=F32)
    ms = jnp.mean(y * y, axis=-1, keepdims=True)
    o_ref[...] = y * lax.rsqrt(ms + EPS) * nw_ref[...]


def ple_proj(p, w, nw, tm=512):
    m, k = p.shape
    n = w.shape[1]
    return pl.pallas_call(
        _ple_proj_body,
        grid=(m // tm,),
        in_specs=[pl.BlockSpec((tm, k), lambda i: (i, 0)),
                  pl.BlockSpec((k, n), lambda i: (0, 0)),
                  pl.BlockSpec((1, n), lambda i: (0, 0))],
        out_specs=pl.BlockSpec((tm, n), lambda i: (i, 0)),
        out_shape=jax.ShapeDtypeStruct((m, n), F32),
        compiler_params=_params("parallel"),
        name="ple_proj",
    )(p, w, nw.reshape(1, n).astype(F32))


def _ffn_body(x_ref, nw_ref, wg_ref, wu_ref, wd_ref, o_ref, h_ref):
    f = pl.program_id(1)

    @pl.when(f == 0)
    def _():
        x = x_ref[...]
        ms = jnp.mean(x * x, axis=-1, keepdims=True)
        h_ref[...] = (x * lax.rsqrt(ms + EPS) * nw_ref[...]).astype(BF16)

        o_ref[...] = jnp.zeros_like(o_ref)

    h = h_ref[...]
    g = jnp.dot(h, wg_ref[...], preferred_element_type=F32)
    u = jnp.dot(h, wu_ref[...], preferred_element_type=F32)
    act = (g * jax.nn.sigmoid(g) * u).astype(BF16)
    o_ref[...] += jnp.dot(act, wd_ref[...], preferred_element_type=F32)

    @pl.when(f == pl.num_programs(1) - 1)
    def _():
        o_ref[...] = x_ref[...] + 0.5 * o_ref[...]


def ffn(x, nw, wg, wu, wd, tm=512, tf=256):
    m, d = x.shape
    dff = wg.shape[1]
    return pl.pallas_call(
        _ffn_body,
        grid=(m // tm, dff // tf),
        in_specs=[pl.BlockSpec((tm, d), lambda i, f: (i, 0), pipeline_mode=pl.Buffered(1)),
                  pl.BlockSpec((1, d), lambda i, f: (0, 0)),
                  pl.BlockSpec((d, tf), lambda i, f: (0, f)),
                  pl.BlockSpec((d, tf), lambda i, f: (0, f)),
                  pl.BlockSpec((tf, d), lambda i, f: (f, 0))],
        out_specs=pl.BlockSpec((tm, d), lambda i, f: (i, 0)),
        out_shape=jax.ShapeDtypeStruct((m, d), F32),
        scratch_shapes=[pltpu.VMEM((tm, d), BF16)],
        compiler_params=_params("parallel", "arbitrary"),
        name="ffn",
    )(x, nw.reshape(1, d).astype(F32), wg, wu, wd)


DSA_KEY_TILE = 256
DSA_HEAD_GROUP = 8
DSA_HEADS_INTERLEAVED = 4
INT32_MIN = -2 ** 31
A_GROUP = A_HEADS // A_KV_HEADS
NT_DIMS = (((1,), (1,)), ((), ()))


def _dsa_body(qi_ref, wi_ref, qt_ref, ki_ref, k_ref, vt_ref, o_ref,
              qi_st, wib, skey, m_s, l_s, acc_s, *, qb, past_len, l_valid, topk):
    lk = DSA_KEY_TILE
    i = pl.program_id(1)
    q_pos0 = past_len + i * qb
    limit = jnp.minimum(((q_pos0 + qb - 1) // CHUNK + 1) * CHUNK, l_valid)
    nt = (limit + lk - 1) // lk

    for h in range(IDX_HEADS):
        qi_st[h * qb:(h + 1) * qb, :] = qi_ref[0, :, h * IDX_DIM:(h + 1) * IDX_DIM]
        wib[h] = jnp.broadcast_to(wi_ref[0, :, h:h + 1], (qb, 128))

    q_chunk = (q_pos0 + lax.broadcasted_iota(jnp.int32, (qb, 128), 0)) // CHUNK

    def score_tile(t, carry):
        kt = ki_ref[0, t]
        cols = [jnp.zeros((qb, 128), F32) for _ in range(lk // 128)]
        for hg in range(IDX_HEADS // DSA_HEAD_GROUP):
            rows = DSA_HEAD_GROUP * qb
            lg = lax.dot_general(qi_st[hg * rows:(hg + 1) * rows, :], kt, NT_DIMS,
                                 preferred_element_type=F32)
            for hh in range(DSA_HEAD_GROUP):
                w = wib[hg * DSA_HEAD_GROUP + hh]
                for c in range(lk // 128):
                    blk = lg[hh * qb:(hh + 1) * qb, c * 128:(c + 1) * 128]
                    cols[c] = cols[c] + jnp.maximum(blk, 0.0) * w
        for c in range(lk // 128):
            k_pos = t * lk + c * 128 + lax.broadcasted_iota(jnp.int32, (qb, 128), 1)
            adm = (k_pos // CHUNK <= q_chunk) & (k_pos < l_valid)
            bits = lax.bitcast_convert_type(cols[c], jnp.int32)
            key = jnp.where(bits < 0, bits ^ jnp.int32(0x7FFFFFFF), bits)
            skey[t, :, c * 128:(c + 1) * 128] = jnp.where(adm, key, jnp.int32(INT32_MIN))
        return carry

    lax.fori_loop(0, nt, score_tile, 0)

    def count_ge(cand):
        def body(t, acc):
            for c in range(lk // 128):
                acc = acc + jnp.where(skey[t, :, c * 128:(c + 1) * 128] >= cand, 1.0, 0.0)
            return acc
        acc = lax.fori_loop(0, nt, body, jnp.zeros((qb, 128), F32))
        return jnp.sum(acc, axis=1, keepdims=True)

    kf = jnp.float32(topk)
    zero = jnp.zeros((qb, 128), jnp.int32)
    prefix = jnp.where(count_ge(zero) >= kf, zero, jnp.int32(INT32_MIN))

    def bit_step(s, prefix):
        cand = prefix + lax.shift_left(jnp.int32(1), jnp.int32(30) - s)
        return jnp.where(count_ge(cand) >= kf, cand, prefix)

    prefix = lax.fori_loop(0, 31, bit_step, prefix)
    thr = jnp.maximum(prefix, jnp.int32(INT32_MIN + 1))

    m_s[...] = jnp.full(m_s.shape, NEG_BIG, F32)
    l_s[...] = jnp.zeros(l_s.shape, F32)
    acc_s[...] = jnp.zeros(acc_s.shape, F32)

    eye_q = jnp.where(lax.broadcasted_iota(jnp.int32, (qb, qb), 0)
                      == lax.broadcasted_iota(jnp.int32, (qb, qb), 1), 1.0, 0.0).astype(BF16)
    eye_d = jnp.where(lax.broadcasted_iota(jnp.int32, (HEAD_DIM, HEAD_DIM), 0)
                      == lax.broadcasted_iota(jnp.int32, (HEAD_DIM, HEAD_DIM), 1), 1.0, 0.0).astype(BF16)

    def attn_tile(t, carry):
        sel = jnp.concatenate(
            [jnp.where(skey[t, :, c * 128:(c + 1) * 128] >= thr, 1.0, 0.0)
             for c in range(lk // 128)], axis=1).astype(BF16)
        sel_t = lax.dot_general(sel, eye_q, TN_DIMS, preferred_element_type=F32)
        bias = (sel_t - 1.0) * (-NEG_BIG)
        bias4 = jnp.concatenate([bias] * A_GROUP, axis=1)
        for n0 in range(0, A_KV_HEADS, DSA_HEADS_INTERLEAVED):
            ns = range(n0, n0 + DSA_HEADS_INTERLEAVED)
            s = [jnp.dot(k_ref[0, t, :, n * HEAD_DIM:(n + 1) * HEAD_DIM], qt_ref[0, 0, n],
                         preferred_element_type=F32) + bias4 for n in ns]
            m_old = [m_s[n] for n in ns]
            m_new = [jnp.maximum(mo, jnp.max(x, axis=0, keepdims=True)) for mo, x in zip(m_old, s)]
            alpha = [jnp.exp(mo - mn) for mo, mn in zip(m_old, m_new)]
            p = [jnp.exp(x - mn) for x, mn in zip(s, m_new)]
            pv = [jnp.dot(vt_ref[0, t, n], x.astype(BF16), preferred_element_type=F32)
                  for n, x in zip(ns, p)]
            for j, n in enumerate(ns):
                l_s[n] = alpha[j] * l_s[n] + jnp.sum(p[j], axis=0, keepdims=True)
                acc_s[n] = alpha[j] * acc_s[n] + pv[j]
                m_s[n] = m_new[j]
        return carry

    lax.fori_loop(0, nt, attn_tile, 0)

    for n in range(A_KV_HEADS):
        o = (acc_s[n] / l_s[n]).astype(BF16)
        for g in range(A_GROUP):
            hd = n * A_GROUP + g
            o_t = lax.dot_general(o[:, g * qb:(g + 1) * qb], eye_d, TN_DIMS,
                                  preferred_element_type=F32)
            o_ref[0, :, hd * HEAD_DIM:(hd + 1) * HEAD_DIM] = o_t.astype(o_ref.dtype)


def dsa_attention_pallas(q, k, v, qi, ki, wi, past_len):
    bsz, t, _ = q.shape
    l = k.shape[1]
    lk = DSA_KEY_TILE
    topk = min(TOPK_MAX, l // 4)
    qb = min(QBLOCK, t)
    n_tiles = -(-l // lk)
    pad = n_tiles * lk - l

    def tiles(a):
        a = jnp.pad(a, ((0, 0), (0, pad), (0, 0))).astype(BF16)
        return a.reshape(bsz, n_tiles, lk, a.shape[-1])

    nb = t // qb
    qs = q.reshape(bsz, nb, qb, A_KV_HEADS, A_GROUP, HEAD_DIM)
    qt = jnp.transpose(qs, (0, 1, 3, 5, 4, 2)).reshape(bsz, nb, A_KV_HEADS, HEAD_DIM, A_GROUP * qb)
    vt = jnp.transpose(tiles(v).reshape(bsz, n_tiles, lk, A_KV_HEADS, HEAD_DIM), (0, 1, 3, 4, 2))
    whole = lambda shape: pl.BlockSpec(shape, lambda b, i: (b,) + (0,) * (len(shape) - 1),
                                       pipeline_mode=pl.Buffered(1))
    rows = lambda width: pl.BlockSpec((1, qb, width), lambda b, i: (b, i, 0))
    body = partial(_dsa_body, qb=qb, past_len=past_len, l_valid=l, topk=topk)
    return pl.pallas_call(
        body,
        grid=(bsz, nb),
        in_specs=[rows(IDX_HEADS * IDX_DIM), rows(IDX_HEADS),
                  pl.BlockSpec((1, 1, A_KV_HEADS, HEAD_DIM, A_GROUP * qb), lambda b, i: (b, i, 0, 0, 0)),
                  whole((1, n_tiles, lk, IDX_DIM)),
                  whole((1, n_tiles, lk, A_KV_HEADS * HEAD_DIM)),
                  whole((1, n_tiles, A_KV_HEADS, HEAD_DIM, lk))],
        out_specs=rows(A_HEADS * HEAD_DIM),
        out_shape=jax.ShapeDtypeStruct((bsz, t, A_HEADS * HEAD_DIM), BF16),
        scratch_shapes=[pltpu.VMEM((IDX_HEADS * qb, IDX_DIM), BF16),
                        pltpu.VMEM((IDX_HEADS, qb, 128), F32),
                        pltpu.VMEM((n_tiles, qb, lk), jnp.int32),
                        pltpu.VMEM((A_KV_HEADS, 1, A_GROUP * qb), F32),
                        pltpu.VMEM((A_KV_HEADS, 1, A_GROUP * qb), F32),
                        pltpu.VMEM((A_KV_HEADS, HEAD_DIM, A_GROUP * qb), F32)],
        compiler_params=_params("parallel", "arbitrary"),
        name="dsa_attention",
    )(qi, wi, qt, tiles(ki), tiles(k), vt)


GLA_LEVELS = 6
GLA_ROWS_PER_STEP = 512


def _gla_exponent_matrix():
    c = CHUNK
    a = np.zeros((GLA_LEVELS + 1, c, c), np.float32)
    a[0] = np.tril(np.ones((c, c), np.float32))
    for l in range(GLA_LEVELS):
        b = (c // 2) >> l
        for t in range(c):
            ref = (t // (2 * b)) * 2 * b + b
            if t >= ref:
                a[l + 1, t, ref + 1:t + 1] = 1.0
            else:
                a[l + 1, t, t + 1:ref + 1] = 1.0
    return a.reshape((GLA_LEVELS + 1) * c, c)


def _split3_dot(a_bf16, x):
    hi = x.astype(BF16)
    r1 = x - hi.astype(F32)
    mid = r1.astype(BF16)
    lo = (r1 - mid.astype(F32)).astype(BF16)
    return (jnp.dot(a_bf16, hi, preferred_element_type=F32)
            + jnp.dot(a_bf16, mid, preferred_element_type=F32)
            + jnp.dot(a_bf16, lo, preferred_element_type=F32))


TN_DIMS = (((0,), (0,)), ((), ()))


def _gla_body(q_ref, f_ref, i_ref, g_ref, lb_ref, nw_ref, a_ref, s0_ref, o_ref, sfin_ref, st_ref,
              *, n_chunks):
    r = pl.program_id(2)
    c = CHUNK

    @pl.when(r == 0)
    def _():
        st_ref[...] = s0_ref[0, 0].T

    lb = lb_ref[...]
    nw = nw_ref[...]
    amat = a_ref[...]
    t_row = lax.broadcasted_iota(jnp.int32, (c, B_DK), 0)
    t_idx = lax.broadcasted_iota(jnp.int32, (c, c), 0)
    s_idx = lax.broadcasted_iota(jnp.int32, (c, c), 1)
    upper = [((t_row >> (GLA_LEVELS - 1 - l)) & 1) == 1 for l in range(GLA_LEVELS)]
    pair = [((t_idx >> (GLA_LEVELS - l)) == (s_idx >> (GLA_LEVELS - l)))
            & (((t_idx >> (GLA_LEVELS - 1 - l)) & 1) == 1)
            & (((s_idx >> (GLA_LEVELS - 1 - l)) & 1) == 0) for l in range(GLA_LEVELS)]
    eye = t_idx == s_idx

    cs = range(n_chunks)
    rows = [slice(ci * c, (ci + 1) * c) for ci in cs]
    k = [(1.0 - lb) * jax.nn.sigmoid(-f_ref[rw, :]) for rw in rows]
    logf = [jnp.log1p(-jnp.minimum(x, 1.0 - 1e-6)) for x in k]
    qr = [q_ref[rw, :] for rw in rows]
    q = [x * jax.nn.sigmoid(x) for x in qr]
    d_all = [_split3_dot(amat, x) for x in logf]
    att = [jnp.where(eye, jnp.sum(a * b, axis=1, keepdims=True), 0.0) for a, b in zip(q, k)]
    for l in range(GLA_LEVELS):
        x = [(jnp.where(upper[l], a, b) * jnp.exp(d[(l + 1) * c:(l + 2) * c])).astype(BF16)
             for a, b, d in zip(q, k, d_all)]
        gram = [lax.dot_general(y, y, NT_DIMS, preferred_element_type=F32) for y in x]
        att = [a + jnp.where(pair[l], g, 0.0) for a, g in zip(att, gram)]
    gcum = [d[0:c] for d in d_all]
    g_last = [x[c - 1:c] for x in gcum]
    q_dec = [(a * jnp.exp(gc)).astype(BF16) for a, gc in zip(q, gcum)]
    k_dec = [(a * jnp.exp(gl - gc)).astype(BF16) for a, gl, gc in zip(k, g_last, gcum)]
    vb = [i_ref[rw, :].astype(BF16) for rw in rows]
    o_intra = [jnp.dot(a.astype(BF16), v, preferred_element_type=F32) for a, v in zip(att, vb)]
    kv = [lax.dot_general(v, kd, TN_DIMS, preferred_element_type=F32) for v, kd in zip(vb, k_dec)]
    e_last = [jnp.exp(x) for x in g_last]

    st = st_ref[...]
    for ci in cs:
        o = o_intra[ci] + lax.dot_general(q_dec[ci], st.astype(BF16), NT_DIMS,
                                          preferred_element_type=F32)
        st = st * e_last[ci] + kv[ci]
        o = o * lax.rsqrt(jnp.mean(o * o, axis=1, keepdims=True) + EPS) * nw
        gr = g_ref[rows[ci], :]
        o_ref[rows[ci], :] = (o * (gr * jax.nn.sigmoid(gr))).astype(o_ref.dtype)
    st_ref[...] = st

    @pl.when(r == pl.num_programs(2) - 1)
    def _():
        sfin_ref[0, 0] = st.T


def hgrn2_pallas(proj_b, row0, bsz, t, lb, norm_w, s0):
    rps = min(t, GLA_ROWS_PER_STEP)
    steps = t // rps
    blk0 = row0 // rps
    seg = lambda s: pl.BlockSpec((rps, B_DK), lambda b, h, r: (blk0 + b * steps + r, s * B_HEADS + h))
    state = pl.BlockSpec((1, 1, B_DK, B_DV), lambda b, h, r: (b, h, 0, 0))
    amat = jnp.asarray(_gla_exponent_matrix(), BF16)
    return pl.pallas_call(
        partial(_gla_body, n_chunks=rps // CHUNK),
        grid=(bsz, B_HEADS, steps),
        in_specs=[seg(0), seg(1), seg(2), seg(3),
                  pl.BlockSpec((1, B_DK), lambda b, h, r: (0, h)),
                  pl.BlockSpec((1, B_DV), lambda b, h, r: (0, 0)),
                  pl.BlockSpec(amat.shape, lambda b, h, r: (0, 0)),
                  state],
        out_specs=[pl.BlockSpec((rps, B_DV), lambda b, h, r: (b * steps + r, h)), state],
        out_shape=[jax.ShapeDtypeStruct((bsz * t, B_HEADS * B_DV), BF16),
                   jax.ShapeDtypeStruct((bsz, B_HEADS, B_DK, B_DV), F32)],
        scratch_shapes=[pltpu.VMEM((B_DV, B_DK), F32)],
        compiler_params=_params("parallel", "parallel", "arbitrary"),
        name="hgrn2",
    )(proj_b, proj_b, proj_b, proj_b, lb.reshape(1, -1).astype(F32),
      norm_w.reshape(1, -1).astype(F32), amat, s0)


GDN_PROBLEMS_PER_STEP = 8
GDN_CHUNKS_PER_STEP = 2


def _split3_dot_tn(x, u_bf16):
    hi = x.astype(BF16)
    r1 = x - hi.astype(F32)
    mid = r1.astype(BF16)
    lo = (r1 - mid.astype(F32)).astype(BF16)
    return (lax.dot_general(hi, u_bf16, TN_DIMS, preferred_element_type=F32)
            + lax.dot_general(mid, u_bf16, TN_DIMS, preferred_element_type=F32)
            + lax.dot_general(lo, u_bf16, TN_DIMS, preferred_element_type=F32))


def _gdn_body(q_ref, k_ref, v_ref, z_ref, beta_ref, g_ref, nw_ref, s0_ref, o_ref, sfin_ref, s_ref,
              *, n_chunks, heads):
    hg = pl.program_id(1)
    r = pl.program_id(2)
    c = CHUNK
    rep = C_V_HEADS // C_QK_HEADS

    @pl.when(r == 0)
    def _():
        s_ref[...] = s0_ref[0]

    nw = nw_ref[...]
    t_idx = lax.broadcasted_iota(jnp.int32, (c, c), 0)
    s_idx = lax.broadcasted_iota(jnp.int32, (c, c), 1)
    incl = s_idx <= t_idx
    strict = s_idx < t_idx
    eye = jnp.where(t_idx == s_idx, 1.0, 0.0)
    tril_b = jnp.where(incl, 1.0, 0.0).astype(BF16)
    triu_b = jnp.where(t_idx <= s_idx, 1.0, 0.0).astype(BF16)
    pair = [((t_idx >> (l + 1)) == (s_idx >> (l + 1)))
            & (((t_idx >> l) & 1) == 1) & (((s_idx >> l) & 1) == 0) for l in range(GLA_LEVELS)]
    lane_head = lax.broadcasted_iota(jnp.int32, (c, C_V_HEADS), 1)

    probs = [(j, ci) for j in range(heads) for ci in range(n_chunks)]
    cs = range(len(probs))
    rows = [slice(ci * c, (ci + 1) * c) for _, ci in probs]
    vcol = [slice(j * C_DV, (j + 1) * C_DV) for j, _ in probs]
    qcol = [slice((j // rep) * C_DK, (j // rep + 1) * C_DK) for j, _ in probs]
    head_sel = [lane_head == hg * heads + j for j, _ in probs]
    qa = [q_ref[rw, qc] for rw, qc in zip(rows, qcol)]
    ka = [k_ref[rw, qc] for rw, qc in zip(rows, qcol)]
    q = [x * lax.rsqrt(jnp.sum(x * x, axis=1, keepdims=True) + EPS) * (C_DK ** -0.5) for x in qa]
    k = [x * lax.rsqrt(jnp.sum(x * x, axis=1, keepdims=True) + EPS) for x in ka]
    beta = [jnp.sum(jnp.where(hs, beta_ref[rw, :], 0.0), axis=1, keepdims=True)
            for rw, hs in zip(rows, head_sel)]
    gb = [jnp.broadcast_to(jnp.sum(jnp.where(hs, g_ref[rw, :], 0.0), axis=1, keepdims=True),
                           (c, C_DK)) for rw, hs in zip(rows, head_sel)]
    gcum = [_split3_dot(tril_b, x) for x in gb]
    gc_row = [_split3_dot_tn(x[:, :c], triu_b) for x in gb]
    decay = [jnp.where(incl, jnp.exp(jnp.where(incl, a[:, :c] - b, 0.0)), 0.0)
             for a, b in zip(gcum, gc_row)]
    kb = [a * b for a, b in zip(k, beta)]
    kbf = [x.astype(BF16) for x in k]
    a_mat = [jnp.where(strict, lax.dot_general(a.astype(BF16), b, NT_DIMS,
                                               preferred_element_type=F32) * d, 0.0)
             for a, b, d in zip(kb, kbf, decay)]
    tinv = [eye for _ in cs]
    for l in range(GLA_LEVELS):
        tb = [x.astype(BF16) for x in tinv]
        xt = [jnp.dot(jnp.where(pair[l], a, 0.0).astype(BF16), b, preferred_element_type=F32)
              for a, b in zip(a_mat, tb)]
        tinv = [t0 - jnp.dot(b, x.astype(BF16), preferred_element_type=F32)
                for t0, b, x in zip(tinv, tb, xt)]
    e_cum = [jnp.exp(x) for x in gcum]
    rhs = [jnp.concatenate([v_ref[rw, vc] * bt, b * e], axis=1)
           for rw, vc, bt, b, e in zip(rows, vcol, beta, kb, e_cum)]
    sol = [x + jnp.dot((t0 - eye).astype(BF16), x.astype(BF16), preferred_element_type=F32)
           for x, t0 in zip(rhs, tinv)]
    qk = [jnp.where(incl, lax.dot_general(a.astype(BF16), b, NT_DIMS,
                                          preferred_element_type=F32) * d, 0.0).astype(BF16)
          for a, b, d in zip(q, kbf, decay)]
    g_last = [x[c - 1:c] for x in gcum]
    q_dec = [(a * e).astype(BF16) for a, e in zip(q, e_cum)]
    k_dec = [(a * jnp.exp(gl - gc)).astype(BF16) for a, gl, gc in zip(k, g_last, gcum)]

    s = [s_ref[j] for j in range(heads)]
    for ci in range(n_chunks):
        ps = [j * n_chunks + ci for j in range(heads)]
        sb = [x.astype(BF16) for x in s]
        ub = [(sol[p][:, :C_DV] - jnp.dot(sol[p][:, C_DV:].astype(BF16), b,
                                          preferred_element_type=F32)).astype(BF16)
              for p, b in zip(ps, sb)]
        o = [jnp.dot(q_dec[p], b, preferred_element_type=F32)
             + jnp.dot(qk[p], u, preferred_element_type=F32) for p, b, u in zip(ps, sb, ub)]
        s = [jnp.exp(g_last[p]) * x + lax.dot_general(k_dec[p], u, TN_DIMS, preferred_element_type=F32)
             for p, x, u in zip(ps, s, ub)]
        for p, x in zip(ps, o):
            x = x * lax.rsqrt(jnp.mean(x * x, axis=1, keepdims=True) + EPS) * nw
            z = z_ref[rows[p], vcol[p]]
            o_ref[rows[p], vcol[p]] = (x * (z * jax.nn.sigmoid(z))).astype(o_ref.dtype)
    for j in range(heads):
        s_ref[j] = s[j]

    @pl.when(r == pl.num_programs(2) - 1)
    def _():
        for j in range(heads):
            sfin_ref[0, j] = s[j]


def gdn_pallas(act, proj_m, beta, g, row0, bsz, t, norm_w, s0):
    n_chunks = min(t // CHUNK, GDN_CHUNKS_PER_STEP)
    heads = GDN_PROBLEMS_PER_STEP // n_chunks
    rps = n_chunks * CHUNK
    steps = t // rps
    blk0 = row0 // rps
    rep = C_V_HEADS // C_QK_HEADS
    qk_w = (heads // rep) * C_DK
    v_w = heads * C_DV
    qk_blocks = C_QK_HEADS * C_DK // qk_w
    qk_spec = lambda seg: pl.BlockSpec((rps, qk_w), lambda b, h, r: (b * steps + r, seg * qk_blocks + h))
    small = pl.BlockSpec((rps, C_V_HEADS), lambda b, h, r: (b * steps + r, 0))
    state = pl.BlockSpec((1, heads, C_DK, C_DV), lambda b, h, r: (b, h, 0, 0))
    return pl.pallas_call(
        partial(_gdn_body, n_chunks=n_chunks, heads=heads),
        grid=(bsz, C_V_HEADS // heads, steps),
        in_specs=[qk_spec(0), qk_spec(1),
                  pl.BlockSpec((rps, v_w), lambda b, h, r: (b * steps + r, 2 * C_QK_HEADS * C_DK // v_w + h)),
                  pl.BlockSpec((rps, v_w), lambda b, h, r: (blk0 + b * steps + r, C_CONV_CH // v_w + h)),
                  small, small,
                  pl.BlockSpec((1, C_DV), lambda b, h, r: (0, 0)),
                  state],
        out_specs=[pl.BlockSpec((rps, v_w), lambda b, h, r: (b * steps + r, h)), state],
        out_shape=[jax.ShapeDtypeStruct((bsz * t, C_V_HEADS * C_DV), BF16),
                   jax.ShapeDtypeStruct((bsz, C_V_HEADS, C_DK, C_DV), F32)],
        scratch_shapes=[pltpu.VMEM((heads, C_DK, C_DV), F32)],
        compiler_params=_params("parallel", "parallel", "arbitrary"),
        name="gated_deltanet",
    )(act, act, act, proj_m, beta, g, norm_w.reshape(1, -1).astype(F32), s0)


def _rms_f32(x, w):
    y = x * lax.rsqrt(jnp.mean(x * x, axis=-1, keepdims=True) + EPS)
    return y * w.astype(F32)


def _l2norm(x):
    return x * lax.rsqrt(jnp.sum(x * x, axis=-1, keepdims=True) + EPS)


def _rope(x, pos, rot_dim):
    half = rot_dim // 2
    inv_freq = ROPE_THETA ** (-jnp.arange(half, dtype=F32) / half)
    ang = pos.astype(F32)[:, None] * inv_freq[None, :]
    cos = jnp.cos(ang)[None, :, None, :]
    sin = jnp.sin(ang)[None, :, None, :]
    x1, x2, rest = x[..., :half], x[..., half:rot_dim], x[..., rot_dim:]
    return jnp.concatenate([x1 * cos - x2 * sin, x2 * cos + x1 * sin, rest], axis=-1)


def _dsa_attention(q, k, v, qi, ki, wi, q_pos, k_pos):
    b, t = q.shape[:2]
    l = k.shape[1]
    topk = min(TOPK_MAX, l // 4)
    qb = min(QBLOCK, t)
    nb = t // qb
    grp = A_HEADS // A_KV_HEADS
    k_chunk = k_pos // CHUNK
    scale = HEAD_DIM ** -0.5

    def blocks(a):
        return jnp.moveaxis(a.reshape((b, nb, qb) + a.shape[2:]), 1, 0)

    def one_block(args):
        q_b, qi_b, wi_b, pos_b = args
        q_chunk = pos_b // CHUNK
        logits = jnp.einsum('bqhd,bkd->bqhk', qi_b, ki).astype(F32)
        score = jnp.einsum('bqh,bqhk->bqk', wi_b, jax.nn.relu(logits)) * IDX_SCALE
        admissible = k_chunk[None, :] <= q_chunk[:, None]
        score = jnp.where(admissible[None], score, NEG_BIG)
        _, top_idx = lax.top_k(score, topk)
        valid = k_chunk[top_idx] <= q_chunk[None, :, None]
        kg = jax.vmap(lambda kk, ii: kk[ii])(k, top_idx)
        vg = jax.vmap(lambda vv, ii: vv[ii])(v, top_idx)
        qg = q_b.reshape(b, qb, A_KV_HEADS, grp, HEAD_DIM)
        s = jnp.einsum('bqngd,bqknd->bqngk', qg, kg).astype(F32) * scale
        s = jnp.where(valid[:, :, None, None, :], s, NEG_BIG)
        p = jax.nn.softmax(s, axis=-1)
        o = jnp.einsum('bqngk,bqknd->bqngd', p, vg)
        return o.reshape(b, qb, A_HEADS, HEAD_DIM)

    out = lax.map(one_block, (blocks(q), blocks(qi), blocks(wi), q_pos.reshape(nb, qb)))
    return jnp.moveaxis(out, 0, 1).reshape(b, t, A_HEADS, HEAD_DIM)


def _gla_chunked(q, k, g, v, s0):
    b, t, h, dk = q.shape
    dv = v.shape[-1]
    c = min(CHUNK, t)
    n = t // c
    causal = jnp.tril(jnp.ones((c, c), dtype=bool))[None, :, :, None, None]

    def cm(a):
        return jnp.moveaxis(a.reshape((b, n, c) + a.shape[2:]), 1, 0)

    def step(s, xs):
        qc, kc, gc, vc = xs
        gcum = jnp.cumsum(gc, axis=1)
        o_inter = jnp.einsum('bthc,bhcv->bthv', qc * jnp.exp(gcum), s)
        diff = gcum[:, :, None] - gcum[:, None, :]
        w = jnp.where(causal, jnp.exp(jnp.where(causal, diff, 0.0)), 0.0)
        att = jnp.einsum('bthc,bshc,btshc->bths', qc, kc, w)
        o_intra = jnp.einsum('bths,bshv->bthv', att, vc)
        g_last = gcum[:, -1]
        s_new = jnp.exp(g_last)[..., None] * s + jnp.einsum(
            'bshc,bshv->bhcv', kc * jnp.exp(g_last[:, None] - gcum), vc)
        return s_new, o_inter + o_intra

    s_fin, o = lax.scan(step, s0, (cm(q), cm(k), cm(g), cm(v)))
    return jnp.moveaxis(o, 0, 1).reshape(b, t, h, dv), s_fin


def _gated_delta_chunked(q, k, v, g, beta, s0):
    b, t, h, dk = q.shape
    dv = v.shape[-1]
    c = min(CHUNK, t)
    n = t // c

    def heads_first(a):
        a = a.reshape((b, n, c, h) + a.shape[3:])
        return jnp.moveaxis(a, 3, 2)

    qh, kh, vh, gh, bh = [heads_first(a) for a in (q, k, v, g, beta)]
    gcum = jnp.cumsum(gh, axis=-1)
    diff = gcum[..., :, None] - gcum[..., None, :]
    incl = jnp.tril(jnp.ones((c, c), dtype=bool))
    strict = jnp.tril(jnp.ones((c, c), dtype=bool), -1)
    decay = jnp.where(incl, jnp.exp(jnp.where(incl, diff, 0.0)), 0.0)
    kb = kh * bh[..., None]
    a_mat = jnp.where(strict, jnp.einsum('bnhtd,bnhsd->bnhts', kb, kh) * decay, 0.0)
    rhs = jnp.concatenate([vh * bh[..., None], kb * jnp.exp(gcum)[..., None]], axis=-1)
    sol = lax.linalg.triangular_solve(a_mat + jnp.eye(c, dtype=a_mat.dtype), rhs,
                                      left_side=True, lower=True, unit_diagonal=True)
    u0, w = sol[..., :dv], sol[..., dv:]
    qk = jnp.einsum('bnhtd,bnhsd->bnhts', qh, kh) * decay
    q_dec = qh * jnp.exp(gcum)[..., None]
    g_last = gcum[..., -1]
    k_dec = kh * jnp.exp(g_last[..., None] - gcum)[..., None]

    def step(s, xs):
        u0c, wc, qkc, qdc, kdc, glc = xs
        u = u0c - jnp.einsum('bhtk,bhkv->bhtv', wc, s)
        o = jnp.einsum('bhtk,bhkv->bhtv', qdc, s) + jnp.einsum('bhts,bhsv->bhtv', qkc, u)
        s = jnp.exp(glc)[..., None, None] * s + jnp.einsum('bhsk,bhsv->bhkv', kdc, u)
        return s, o

    xs = tuple(jnp.moveaxis(a, 1, 0) for a in (u0, w, qk, q_dec, k_dec, g_last))
    s_fin, o = lax.scan(step, s0, xs)
    o = jnp.transpose(o, (1, 0, 3, 2, 4)).reshape(b, t, h, dv)
    return o, s_fin


def _hgrn2_mixer(q_raw, f_raw, i_raw, g_raw, lb, norm_w, s0):
    b, t, _ = q_raw.shape
    k = (1.0 - lb) * jax.nn.sigmoid(-f_raw)
    log_f = jnp.log1p(-jnp.minimum(k, 1.0 - 1e-6))
    q = jax.nn.silu(q_raw)
    shp = (b, t, B_HEADS, B_DK)
    o, s_fin = _gla_chunked(q.reshape(shp), k.reshape(shp), log_f.reshape(shp),
                            i_raw.reshape(b, t, B_HEADS, B_DV), s0)
    o = _rms_f32(o, norm_w) * jax.nn.silu(g_raw.reshape(b, t, B_HEADS, B_DV))
    return o.reshape(b, t, B_HEADS * B_DV), s_fin


EVEN_A_COLS = 7680
OFF_AK, OFF_AV, OFF_IQ, OFF_IK, OFF_IW = 2048, 2560, 3072, 7168, 7296


def _rope_tables(pos):
    posf = pos.astype(F32)[:, None]

    def cs(half):
        inv_freq = ROPE_THETA ** (-jnp.arange(half, dtype=F32) / half)
        ang = posf * inv_freq[None, :]
        return jnp.cos(ang), jnp.sin(ang)

    c64, s64 = cs(HEAD_DIM // 2)
    c32, s32 = cs(IDX_ROPE_DIM // 2)
    z32 = jnp.zeros_like(s32)
    rest = IDX_DIM - IDX_ROPE_DIM
    ones = jnp.ones((pos.shape[0], rest), F32)
    zeros = jnp.zeros((pos.shape[0], rest), F32)
    return (jnp.concatenate([c64, c64], 1), jnp.concatenate([-s64, s64], 1),
            jnp.concatenate([c32, c32, ones], 1), jnp.concatenate([-s32, z32, zeros], 1),
            jnp.concatenate([z32, s32, zeros], 1))


def _even_prep_body(p_ref, ca_ref, sa_ref, ci_ref, s1_ref, s2_ref, qn_ref, kn_ref,
                    q_ref, k_ref, kb_ref, v_ref, vb_ref, iq_ref, ik_ref, ikb_ref, wi_ref):
    ca, sa = ca_ref[...], sa_ref[...]
    ci, s1, s2 = ci_ref[...], s1_ref[...], s2_ref[...]

    def head(off, j):
        return p_ref[:, off + j * HEAD_DIM:off + (j + 1) * HEAD_DIM]

    def norm_rope(x, w):
        y = x * lax.rsqrt(jnp.mean(x * x, axis=1, keepdims=True) + EPS) * w
        return y * ca + pltpu.roll(y, HEAD_DIM // 2, 1) * sa

    def idx_rope(x):
        half = IDX_ROPE_DIM // 2
        return x * ci + pltpu.roll(x, IDX_DIM - half, 1) * s1 + pltpu.roll(x, half, 1) * s2

    qn, kn = qn_ref[...], kn_ref[...]
    for j in range(A_HEADS):
        cols = slice(j * HEAD_DIM, (j + 1) * HEAD_DIM)
        q_ref[:, cols] = (norm_rope(head(0, j), qn) * (HEAD_DIM ** -0.5)).astype(BF16)
    for j in range(A_KV_HEADS):
        cols = slice(j * HEAD_DIM, (j + 1) * HEAD_DIM)
        kk = norm_rope(head(OFF_AK, j), kn)
        k_ref[:, cols] = kk
        kb_ref[:, cols] = kk.astype(BF16)
        vv = head(OFF_AV, j)
        v_ref[:, cols] = vv
        vb_ref[:, cols] = vv.astype(BF16)
    for j in range(IDX_HEADS):
        cols = slice(j * IDX_DIM, (j + 1) * IDX_DIM)
        iq_ref[:, cols] = idx_rope(head(OFF_IQ, j)).astype(BF16)
    ik = idx_rope(head(OFF_IK, 0))
    ik_ref[...] = ik
    ikb_ref[...] = ik.astype(BF16)
    wi_ref[...] = p_ref[:, OFF_IW:OFF_IW + IDX_HEADS] * IDX_SCALE


def even_prep(proj_a, pos, q_norm, k_norm, tm=256):
    m = proj_a.shape[0]
    tabs = _rope_tables(pos)
    row = lambda w: pl.BlockSpec((tm, w), lambda i: (i, 0))
    vec = pl.BlockSpec((1, HEAD_DIM), lambda i: (0, 0))
    kvw = A_KV_HEADS * HEAD_DIM
    shapes = [(A_HEADS * HEAD_DIM, BF16), (kvw, F32), (kvw, BF16), (kvw, F32), (kvw, BF16),
              (IDX_HEADS * IDX_DIM, BF16), (IDX_DIM, F32), (IDX_DIM, BF16), (IDX_HEADS, F32)]
    return pl.pallas_call(
        _even_prep_body,
        grid=(m // tm,),
        in_specs=[row(EVEN_A_COLS)] + [row(HEAD_DIM)] * 5 + [vec, vec],
        out_specs=[row(w) for w, _ in shapes],
        out_shape=[jax.ShapeDtypeStruct((m, w), dt) for w, dt in shapes],
        compiler_params=_params("parallel"),
        name="even_prep",
    )(proj_a, *tabs, q_norm.reshape(1, -1).astype(F32), k_norm.reshape(1, -1).astype(F32))


def _even_mixer(prep, proj_b, row0, bsz, t, past, lb, bnorm):
    m = bsz * t
    q_bf, k_f, k_bf, v_f, v_bf, iq_bf, ik_f, ik_bf, wi = [a[row0:row0 + m].reshape(bsz, t, -1) for a in prep]
    past_len = 0 if past is None else past[0].shape[1]
    if past is None:
        keys, vals, ikeys = k_bf, v_bf, ik_bf
        s0_b = jnp.zeros((bsz, B_HEADS, B_DK, B_DV), F32)
    else:
        flat = lambda a: a.reshape(bsz, past_len, -1).astype(BF16)
        keys = jnp.concatenate([flat(past[0]), k_bf], axis=1)
        vals = jnp.concatenate([flat(past[1]), v_bf], axis=1)
        ikeys = jnp.concatenate([flat(past[2]), ik_bf], axis=1)
        s0_b = past[3]
    o_a = dsa_attention_pallas(q_bf, keys, vals, iq_bf, ikeys, wi, past_len)
    o_b, s_b = hgrn2_pallas(proj_b, row0, bsz, t, lb, bnorm, s0_b)
    mixed_in = jnp.concatenate([o_a.reshape(m, -1), o_b], axis=-1)
    new_k = k_f.reshape(bsz, t, A_KV_HEADS, HEAD_DIM)
    new_v = v_f.reshape(bsz, t, A_KV_HEADS, HEAD_DIM)
    return mixed_in, (new_k, new_v, ik_f, s_b)


CONV_HALO = 8


def _conv_body(x_ref, st_ref, w_ref, o_ref, carry_ref):
    r = pl.program_id(2)
    x = x_ref[...]
    tm = x.shape[0]
    prev = jnp.where(r == 0, st_ref[0], carry_ref[...])
    top = x[0:CONV_HALO]
    row = lax.broadcasted_iota(jnp.int32, top.shape, 0)
    acc = x * w_ref[C_CONV - 1:C_CONV]
    acc_top = top * w_ref[C_CONV - 1:C_CONV]
    for k in range(1, C_CONV):
        w = w_ref[C_CONV - 1 - k:C_CONV - k]
        acc = acc + pltpu.roll(x, k, 0) * w
        acc_top = acc_top + jnp.where(row >= k, pltpu.roll(top, k, 0), pltpu.roll(prev, k, 0)) * w
    o_ref[...] = acc * jax.nn.sigmoid(acc)
    o_ref[0:CONV_HALO] = acc_top * jax.nn.sigmoid(acc_top)
    carry_ref[...] = x[tm - CONV_HALO:tm]


def conv_silu(proj_m, row0, bsz, t, conv_state, conv_w, tc=1024):
    tm = min(t, 512)
    steps = t // tm
    blk0 = row0 // tm
    st8 = jnp.pad(conv_state, ((0, 0), (CONV_HALO - (C_CONV - 1), 0), (0, 0)))
    return pl.pallas_call(
        _conv_body,
        grid=(C_CONV_CH // tc, bsz, steps),
        in_specs=[pl.BlockSpec((tm, tc), lambda c, b, r: (blk0 + b * steps + r, c)),
                  pl.BlockSpec((1, CONV_HALO, tc), lambda c, b, r: (b, 0, c)),
                  pl.BlockSpec((C_CONV, tc), lambda c, b, r: (0, c))],
        out_specs=pl.BlockSpec((tm, tc), lambda c, b, r: (b * steps + r, c)),
        out_shape=jax.ShapeDtypeStruct((bsz * t, C_CONV_CH), F32),
        scratch_shapes=[pltpu.VMEM((CONV_HALO, tc), F32)],
        compiler_params=_params("parallel", "parallel", "arbitrary"),
        name="conv_silu",
    )(proj_m, st8, conv_w.astype(F32))


def _even_mixer_unused(proj_a, proj_b, row0, bsz, t, past, aqn, akn, lb, bnorm):
    pa = proj_a.reshape(bsz, t, -1)
    aq, ak, av = pa[..., :2048], pa[..., 2048:2560], pa[..., 2560:3072]
    iq, ik, iw = pa[..., 3072:7168], pa[..., 7168:7296], pa[..., 7296:7328]
    past_len = 0 if past is None else past[0].shape[1]
    q_pos = past_len + jnp.arange(t, dtype=jnp.int32)
    aq = _rope(_rms_f32(aq.reshape(bsz, t, A_HEADS, HEAD_DIM), aqn), q_pos, HEAD_DIM)
    ak = _rope(_rms_f32(ak.reshape(bsz, t, A_KV_HEADS, HEAD_DIM), akn), q_pos, HEAD_DIM)
    av = av.reshape(bsz, t, A_KV_HEADS, HEAD_DIM)
    iq = _rope(iq.reshape(bsz, t, IDX_HEADS, IDX_DIM), q_pos, IDX_ROPE_DIM)
    ik = _rope(ik.reshape(bsz, t, 1, IDX_DIM), q_pos, IDX_ROPE_DIM)[:, :, 0]
    if past is None:
        keys, vals, ikeys = ak, av, ik
        s0_b = jnp.zeros((bsz, B_HEADS, B_DK, B_DV), F32)
    else:
        keys = jnp.concatenate([past[0], ak], axis=1)
        vals = jnp.concatenate([past[1], av], axis=1)
        ikeys = jnp.concatenate([past[2], ik], axis=1)
        s0_b = past[3]
    lkeys = keys.shape[1]
    o_a = dsa_attention_pallas(aq.reshape(bsz, t, -1), keys.reshape(bsz, lkeys, -1),
                               vals.reshape(bsz, lkeys, -1), iq.reshape(bsz, t, -1), ikeys, iw,
                               past_len)
    o_b, s_b = hgrn2_pallas(proj_b, row0, bsz, t, lb, bnorm, s0_b)
    mixed_in = jnp.concatenate([o_a.reshape(bsz * t, -1), o_b], axis=-1)
    return mixed_in, (ak, av, ik, s_b)


def _odd_mixer(proj_m, proj_s, row0, bsz, t, past, conv_w, a_log, dt_bias, onorm):
    m = bsz * t
    ps = proj_s[row0:row0 + m]
    b_raw, a_raw = ps[:, :C_V_HEADS], ps[:, C_V_HEADS:2 * C_V_HEADS]
    if past is None:
        conv_state = jnp.zeros((bsz, C_CONV - 1, C_CONV_CH), F32)
        s0_c = jnp.zeros((bsz, C_V_HEADS, C_DK, C_DV), F32)
    else:
        s0_c, conv_state = past
    act = conv_silu(proj_m, row0, bsz, t, conv_state, conv_w)
    tail = [proj_m[row0 + b * t + t - (C_CONV - 1):row0 + (b + 1) * t, :C_CONV_CH] for b in range(bsz)]
    new_conv = jnp.stack(tail)
    beta = jax.nn.sigmoid(b_raw)
    g = -jnp.exp(a_log.astype(F32)) * jax.nn.softplus(a_raw + dt_bias.astype(F32))
    o_c, s_c = gdn_pallas(act, proj_m, beta, g, row0, bsz, t, onorm, s0_c)
    return o_c, (s_c, new_conv)


def _pad_cols(w, n):
    return jnp.pad(w, ((0, 0), (0, n - w.shape[1])))


def kernel(x_prompt, x_sample, cache_a_k, cache_a_v, cache_a_kidx, state_b, state_c, state_c_conv, p_prompt, p_sample, ffn1_norm, ffn1_w_gate, ffn1_w_up, ffn1_w_down, mix_norm, even_w_in, even_w_out, a_q_norm, a_k_norm, b_lb_logits, b_out_norm, odd_w_in, odd_w_out, c_conv_w, c_a_log, c_dt_bias, c_out_norm, ffn2_norm, ffn2_w_gate, ffn2_w_up, ffn2_w_down, ple_norm, ple_w_gate, ple_w_proj, ple_post_norm):
    d = D_MODEL
    bp, tp = x_prompt.shape[:2]
    bs, ts = x_sample.shape[:2]
    mp, ms = bp * tp, bs * ts
    x = jnp.concatenate([x_prompt.reshape(mp, d), x_sample.reshape(ms, d)], axis=0)
    p_all = jnp.concatenate([p_prompt.reshape(DEPTH, mp, -1), p_sample.reshape(DEPTH, ms, -1)],
                            axis=1).astype(BF16)

    past_len = cache_a_k.shape[2]
    pos_all = jnp.concatenate([jnp.tile(jnp.arange(tp, dtype=jnp.int32), bp),
                               jnp.tile(past_len + jnp.arange(ts, dtype=jnp.int32), bs)])

    lb_soft = jax.nn.softmax(b_lb_logits.astype(F32), axis=0)
    lbs = jnp.cumsum(lb_soft, axis=0) - lb_soft[0]

    new_p = [[] for _ in range(6)]
    new_s = [[] for _ in range(6)]
    for layer in range(DEPTH):
        j = layer // 2
        x = ffn(x, ffn1_norm[layer], ffn1_w_gate[layer].astype(BF16),
                ffn1_w_up[layer].astype(BF16), ffn1_w_down[layer].astype(BF16))
        h = rmsnorm(x, mix_norm[layer])
        if layer % 2 == 0:
            w_in = even_w_in[j]
            w_a = _pad_cols(w_in[:, :7328], 7680).astype(BF16)
            w_b = w_in[:, 7328:].astype(BF16)
            proj_a = matmul(h, w_a, name="even_in_a")
            proj_b = matmul(h, w_b, name="even_in_b")
            prep = even_prep(proj_a, pos_all, a_q_norm[j], a_k_norm[j])
            mix_p, st_p = _even_mixer(prep, proj_b, 0, bp, tp, None, lbs[j], b_out_norm[j])
            mix_s, st_s = _even_mixer(prep, proj_b, mp, bs, ts,
                                      (cache_a_k[j], cache_a_v[j], cache_a_kidx[j], state_b[j]),
                                      lbs[j], b_out_norm[j])
            for idx in range(4):
                new_p[idx].append(st_p[idx])
                new_s[idx].append(st_s[idx])
            w_out = even_w_out[j].astype(BF16)
        else:
            w_in = odd_w_in[j]
            w_m = w_in[:, :12288].astype(BF16)
            w_s = _pad_cols(w_in[:, 12288:], 128).astype(BF16)
            proj_m = matmul(h, w_m, name="odd_in_m")
            proj_s = matmul(h, w_s, tn=128, name="odd_in_s")
            mix_p, st_p = _odd_mixer(proj_m, proj_s, 0, bp, tp, None,
                                     c_conv_w[j], c_a_log[j], c_dt_bias[j], c_out_norm[j])
            mix_s, st_s = _odd_mixer(proj_m, proj_s, mp, bs, ts,
                                     (state_c[j], state_c_conv[j]),
                                     c_conv_w[j], c_a_log[j], c_dt_bias[j], c_out_norm[j])
            for idx in range(2):
                new_p[4 + idx].append(st_p[idx])
                new_s[4 + idx].append(st_s[idx])
            w_out = odd_w_out[j].astype(BF16)
        mix_in = jnp.concatenate([mix_p, mix_s], axis=0).astype(BF16)
        x = matmul(mix_in, w_out, x, body=_mm_resid_body, name="mix_out")
        x = ffn(x, ffn2_norm[layer], ffn2_w_gate[layer].astype(BF16),
                ffn2_w_up[layer].astype(BF16), ffn2_w_down[layer].astype(BF16))
        hp = rmsnorm(x, ple_norm[layer])
        pp = ple_proj(p_all[layer], ple_w_proj[layer].astype(BF16), ple_post_norm[layer])
        x = matmul(hp, ple_w_gate[layer].astype(BF16), x, pp, body=_mm_gate_body, name="ple_gate")

    y_prompt = x[:mp].reshape(bp, tp, d)
    y_sample = x[mp:].reshape(bs, ts, d)
    outs_p = tuple(jnp.stack(v) for v in new_p)
    outs_s = tuple(jnp.stack(v) for v in new_s)
    return (y_prompt, y_sample) + outs_p + outs_s
```

```python
from functools import partial

import numpy as np

import jax
import jax.numpy as jnp
from jax import lax
from jax.experimental import pallas as pl
from jax.experimental.pallas import tpu as pltpu

D_MODEL = 4096
DEPTH = 4
CHUNK = 64
QBLOCK = 128
EPS = 1e-6
NEG_BIG = -1e30
ROPE_THETA = 10000.0
HEAD_DIM = 128
A_HEADS = 16
A_KV_HEADS = 4
IDX_HEADS = 32
IDX_DIM = 128
IDX_ROPE_DIM = 64
TOPK_MAX = 256
IDX_SCALE = (IDX_HEADS * IDX_DIM) ** -0.5
B_HEADS = 16
B_DK = 128
B_DV = 128
C_QK_HEADS = 16
C_V_HEADS = 32
C_DK = 128
C_DV = 128
C_CONV = 4
C_CONV_CH = 2 * C_QK_HEADS * C_DK + C_V_HEADS * C_DV

V7X_VMEM_LIMIT_BYTES = 56 * 1024 * 1024

BF16 = jnp.bfloat16
F32 = jnp.float32


def _params(*sem):
    return pltpu.CompilerParams(dimension_semantics=sem,
                                vmem_limit_bytes=V7X_VMEM_LIMIT_BYTES)


def _rmsnorm_body(x_ref, w_ref, o_ref):
    x = x_ref[...]
    ms = jnp.mean(x * x, axis=-1, keepdims=True)
    o_ref[...] = (x * lax.rsqrt(ms + EPS) * w_ref[...]).astype(o_ref.dtype)


def rmsnorm(x, w, tm=512, out_dtype=BF16):
    m, d = x.shape
    return pl.pallas_call(
        _rmsnorm_body,
        grid=(m // tm,),
        in_specs=[pl.BlockSpec((tm, d), lambda i: (i, 0)),
                  pl.BlockSpec((1, d), lambda i: (0, 0))],
        out_specs=pl.BlockSpec((tm, d), lambda i: (i, 0)),
        out_shape=jax.ShapeDtypeStruct((m, d), out_dtype),
        compiler_params=_params("parallel"),
        name="rmsnorm",
    )(x, w.reshape(1, d).astype(F32))


def _mm_body(a_ref, w_ref, o_ref):
    o_ref[...] = jnp.dot(a_ref[...], w_ref[...], preferred_element_type=F32)


def _mm_resid_body(a_ref, w_ref, r_ref, o_ref):
    o_ref[...] = r_ref[...] + jnp.dot(a_ref[...], w_ref[...], preferred_element_type=F32)


def _mm_gate_body(a_ref, w_ref, r_ref, p_ref, o_ref):
    acc = jnp.dot(a_ref[...], w_ref[...], preferred_element_type=F32)
    o_ref[...] = r_ref[...] + jax.nn.sigmoid(acc) * p_ref[...]


def matmul(a, w, *extras, tm=1024, tn=512, body=_mm_body, name="matmul"):
    m, k = a.shape
    n = w.shape[1]
    tile = pl.BlockSpec((tm, tn), lambda i, j: (i, j))
    return pl.pallas_call(
        body,
        grid=(m // tm, n // tn),
        in_specs=[pl.BlockSpec((tm, k), lambda i, j: (i, 0)),
                  pl.BlockSpec((k, tn), lambda i, j: (0, j))] + [tile] * len(extras),
        out_specs=tile,
        out_shape=jax.ShapeDtypeStruct((m, n), F32),
        compiler_params=_params("parallel", "arbitrary"),
        name=name,
    )(a, w, *extras)


def _ple_proj_body(p_ref, w_ref, nw_ref, o_ref):
    y = jnp.dot(p_ref[...], w_ref[...], preferred_element_type=F32)
    ms = jnp.mean(y * y, axis=-1, keepdims=True)
    o_ref[...] = y * lax.rsqrt(ms + EPS) * nw_ref[...]


def ple_proj(p, w, nw, tm=512):
    m, k = p.shape
    n = w.shape[1]
    return pl.pallas_call(
        _ple_proj_body,
        grid=(m // tm,),
        in_specs=[pl.BlockSpec((tm, k), lambda i: (i, 0)),
                  pl.BlockSpec((k, n), lambda i: (0, 0)),
                  pl.BlockSpec((1, n), lambda i: (0, 0))],
        out_specs=pl.BlockSpec((tm, n), lambda i: (i, 0)),
        out_shape=jax.ShapeDtypeStruct((m, n), F32),
        compiler_params=_params("parallel"),
        name="ple_proj",
    )(p, w, nw.reshape(1, n).astype(F32))


def _ffn_body(x_ref, nw_ref, wg_ref, wu_ref, wd_ref, o_ref, h_ref):
    f = pl.program_id(1)

    @pl.when(f == 0)
    def _():
        x = x_ref[...]
        ms = jnp.mean(x * x, axis=-1, keepdims=True)
        h_ref[...] = (x * lax.rsqrt(ms + EPS) * nw_ref[...]).astype(BF16)

        o_ref[...] = jnp.zeros_like(o_ref)

    h = h_ref[...]
    g = jnp.dot(h, wg_ref[...], preferred_element_type=F32)
    u = jnp.dot(h, wu_ref[...], preferred_element_type=F32)
    act = (g * jax.nn.sigmoid(g) * u).astype(BF16)
    o_ref[...] += jnp.dot(act, wd_ref[...], preferred_element_type=F32)

    @pl.when(f == pl.num_programs(1) - 1)
    def _():
        o_ref[...] = x_ref[...] + 0.5 * o_ref[...]


def ffn(x, nw, wg, wu, wd, tm=512, tf=256):
    m, d = x.shape
    dff = wg.shape[1]
    return pl.pallas_call(
        _ffn_body,
        grid=(m // tm, dff // tf),
        in_specs=[pl.BlockSpec((tm, d), lambda i, f: (i, 0), pipeline_mode=pl.Buffered(1)),
                  pl.BlockSpec((1, d), lambda i, f: (0, 0)),
                  pl.BlockSpec((d, tf), lambda i, f: (0, f)),
                  pl.BlockSpec((d, tf), lambda i, f: (0, f)),
                  pl.BlockSpec((tf, d), lambda i, f: (f, 0))],
        out_specs=pl.BlockSpec((tm, d), lambda i, f: (i, 0)),
        out_shape=jax.ShapeDtypeStruct((m, d), F32),
        scratch_shapes=[pltpu.VMEM((tm, d), BF16)],
        compiler_params=_params("parallel", "arbitrary"),
        name="ffn",
    )(x, nw.reshape(1, d).astype(F32), wg, wu, wd)


DSA_KEY_TILE = 256
DSA_HEAD_GROUP = 8
DSA_HEADS_INTERLEAVED = 4
INT32_MIN = -2 ** 31
A_GROUP = A_HEADS // A_KV_HEADS
NT_DIMS = (((1,), (1,)), ((), ()))


def _dsa_body(qi_ref, wi_ref, qt_ref, ki_ref, k_ref, vt_ref, o_ref,
              qi_st, wib, skey, m_s, l_s, acc_s, *, qb, past_len, l_valid, topk, idx_bits):
    lk = DSA_KEY_TILE
    i = pl.program_id(1)
    q_pos0 = past_len + i * qb
    limit = jnp.minimum(((q_pos0 + qb - 1) // CHUNK + 1) * CHUNK, l_valid)
    nt = (limit + lk - 1) // lk

    for h in range(IDX_HEADS):
        qi_st[h * qb:(h + 1) * qb, :] = qi_ref[0, :, h * IDX_DIM:(h + 1) * IDX_DIM]
        wib[h] = jnp.broadcast_to(wi_ref[0, :, h:h + 1], (qb, 128))

    q_chunk = (q_pos0 + lax.broadcasted_iota(jnp.int32, (qb, 128), 0)) // CHUNK

    def score_tile(t, carry):
        kt = ki_ref[0, t]
        cols = [jnp.zeros((qb, 128), F32) for _ in range(lk // 128)]
        for hg in range(IDX_HEADS // DSA_HEAD_GROUP):
            rows = DSA_HEAD_GROUP * qb
            lg = lax.dot_general(qi_st[hg * rows:(hg + 1) * rows, :], kt, NT_DIMS,
                                 preferred_element_type=F32)
            for hh in range(DSA_HEAD_GROUP):
                w = wib[hg * DSA_HEAD_GROUP + hh]
                for c in range(lk // 128):
                    blk = lg[hh * qb:(hh + 1) * qb, c * 128:(c + 1) * 128]
                    cols[c] = cols[c] + jnp.maximum(blk, 0.0) * w
        for c in range(lk // 128):
            k_pos = t * lk + c * 128 + lax.broadcasted_iota(jnp.int32, (qb, 128), 1)
            adm = (k_pos // CHUNK <= q_chunk) & (k_pos < l_valid)
            bits = lax.bitcast_convert_type(cols[c], jnp.int32)
            key = jnp.where(bits < 0, bits ^ jnp.int32(0x7FFFFFFF), bits)
            skey[t, :, c * 128:(c + 1) * 128] = jnp.where(adm, key, jnp.int32(INT32_MIN))
        return carry

    lax.fori_loop(0, nt, score_tile, 0)

    def count_ge(cand):
        def body(t, acc):
            for c in range(lk // 128):
                acc = acc + jnp.where(skey[t, :, c * 128:(c + 1) * 128] >= cand, 1.0, 0.0)
            return acc
        acc = lax.fori_loop(0, nt, body, jnp.zeros((qb, 128), F32))
        return jnp.sum(acc, axis=1, keepdims=True)

    kf = jnp.float32(topk)
    zero = jnp.zeros((qb, 128), jnp.int32)
    prefix = jnp.where(count_ge(zero) >= kf, zero, jnp.int32(INT32_MIN))

    def bit_step(s, prefix):
        cand = prefix + lax.shift_left(jnp.int32(1), jnp.int32(30) - s)
        return jnp.where(count_ge(cand) >= kf, cand, prefix)

    prefix = lax.fori_loop(0, 31, bit_step, prefix)
    thr = jnp.maximum(prefix, jnp.int32(INT32_MIN + 1))

    real = prefix > jnp.int32(INT32_MIN)
    tie_rows = jnp.where((count_ge(prefix) > kf) & real[:, 0:1], 1.0, 0.0)

    @pl.when(jnp.max(tie_rows) > 0.0)
    def _():
        lane = lax.broadcasted_iota(jnp.int32, (qb, 128), 1)

        def count_tiles(pred):
            def body(t, acc):
                for c in range(lk // 128):
                    tile = skey[t, :, c * 128:(c + 1) * 128]
                    acc = acc + jnp.where(pred(tile, t * lk + c * 128 + lane), 1.0, 0.0)
                return acc
            acc = lax.fori_loop(0, nt, body, jnp.zeros((qb, 128), F32))
            return jnp.sum(acc, axis=1, keepdims=True)

        need = kf - count_tiles(lambda tile, kidx: tile > prefix)

        def cut_step(s, cut):
            cand = cut + lax.shift_left(jnp.int32(1), jnp.int32(idx_bits - 1) - s)
            before = count_tiles(lambda tile, kidx: (tile == prefix) & (kidx < cand))
            return jnp.where(before < need, cand, cut)

        cut = lax.fori_loop(0, idx_bits, cut_step, jnp.zeros((qb, 128), jnp.int32))

        def demote(t, carry):
            for c in range(lk // 128):
                cols = slice(c * 128, (c + 1) * 128)
                tile = skey[t, :, cols]
                drop = real & (tile == prefix) & (t * lk + c * 128 + lane > cut)
                skey[t, :, cols] = jnp.where(drop, prefix - 1, tile)
            return carry

        lax.fori_loop(0, nt, demote, 0)

    m_s[...] = jnp.full(m_s.shape, NEG_BIG, F32)
    l_s[...] = jnp.zeros(l_s.shape, F32)
    acc_s[...] = jnp.zeros(acc_s.shape, F32)

    eye_q = jnp.where(lax.broadcasted_iota(jnp.int32, (qb, qb), 0)
                      == lax.broadcasted_iota(jnp.int32, (qb, qb), 1), 1.0, 0.0).astype(BF16)
    eye_d = jnp.where(lax.broadcasted_iota(jnp.int32, (HEAD_DIM, HEAD_DIM), 0)
                      == lax.broadcasted_iota(jnp.int32, (HEAD_DIM, HEAD_DIM), 1), 1.0, 0.0).astype(BF16)

    def attn_tile(t, carry):
        sel = jnp.concatenate(
            [jnp.where(skey[t, :, c * 128:(c + 1) * 128] >= thr, 1.0, 0.0)
             for c in range(lk // 128)], axis=1).astype(BF16)
        sel_t = lax.dot_general(sel, eye_q, TN_DIMS, preferred_element_type=F32)
        bias = (sel_t - 1.0) * (-NEG_BIG)
        bias4 = jnp.concatenate([bias] * A_GROUP, axis=1)
        for n0 in range(0, A_KV_HEADS, DSA_HEADS_INTERLEAVED):
            ns = range(n0, n0 + DSA_HEADS_INTERLEAVED)
            s = [jnp.dot(k_ref[0, t, :, n * HEAD_DIM:(n + 1) * HEAD_DIM], qt_ref[0, 0, n],
                         preferred_element_type=F32) + bias4 for n in ns]
            m_old = [m_s[n] for n in ns]
            m_new = [jnp.maximum(mo, jnp.max(x, axis=0, keepdims=True)) for mo, x in zip(m_old, s)]
            alpha = [jnp.exp(mo - mn) for mo, mn in zip(m_old, m_new)]
            p = [jnp.exp(x - mn) for x, mn in zip(s, m_new)]
            pv = [jnp.dot(vt_ref[0, t, n], x.astype(BF16), preferred_element_type=F32)
                  for n, x in zip(ns, p)]
            for j, n in enumerate(ns):
                l_s[n] = alpha[j] * l_s[n] + jnp.sum(p[j], axis=0, keepdims=True)
                acc_s[n] = alpha[j] * acc_s[n] + pv[j]
                m_s[n] = m_new[j]
        return carry

    lax.fori_loop(0, nt, attn_tile, 0)

    for n in range(A_KV_HEADS):
        o = (acc_s[n] / l_s[n]).astype(BF16)
        for g in range(A_GROUP):
            hd = n * A_GROUP + g
            o_t = lax.dot_general(o[:, g * qb:(g + 1) * qb], eye_d, TN_DIMS,
                                  preferred_element_type=F32)
            o_ref[0, :, hd * HEAD_DIM:(hd + 1) * HEAD_DIM] = o_t.astype(o_ref.dtype)


def dsa_attention_pallas(q, k, v, qi, ki, wi, past_len):
    bsz, t, _ = q.shape
    l = k.shape[1]
    lk = DSA_KEY_TILE
    topk = min(TOPK_MAX, l // 4)
    qb = min(QBLOCK, t)
    n_tiles = -(-l // lk)
    pad = n_tiles * lk - l

    def tiles(a):
        a = jnp.pad(a, ((0, 0), (0, pad), (0, 0))).astype(BF16)
        return a.reshape(bsz, n_tiles, lk, a.shape[-1])

    nb = t // qb
    qs = q.reshape(bsz, nb, qb, A_KV_HEADS, A_GROUP, HEAD_DIM)
    qt = jnp.transpose(qs, (0, 1, 3, 5, 4, 2)).reshape(bsz, nb, A_KV_HEADS, HEAD_DIM, A_GROUP * qb)
    vt = jnp.transpose(tiles(v).reshape(bsz, n_tiles, lk, A_KV_HEADS, HEAD_DIM), (0, 1, 3, 4, 2))
    whole = lambda shape: pl.BlockSpec(shape, lambda b, i: (b,) + (0,) * (len(shape) - 1),
                                       pipeline_mode=pl.Buffered(1))
    rows = lambda width: pl.BlockSpec((1, qb, width), lambda b, i: (b, i, 0))
    body = partial(_dsa_body, qb=qb, past_len=past_len, l_valid=l, topk=topk,
                   idx_bits=(n_tiles * lk - 1).bit_length())
    return pl.pallas_call(
        body,
        grid=(bsz, nb),
        in_specs=[rows(IDX_HEADS * IDX_DIM), rows(IDX_HEADS),
                  pl.BlockSpec((1, 1, A_KV_HEADS, HEAD_DIM, A_GROUP * qb), lambda b, i: (b, i, 0, 0, 0)),
                  whole((1, n_tiles, lk, IDX_DIM)),
                  whole((1, n_tiles, lk, A_KV_HEADS * HEAD_DIM)),
                  whole((1, n_tiles, A_KV_HEADS, HEAD_DIM, lk))],
        out_specs=rows(A_HEADS * HEAD_DIM),
        out_shape=jax.ShapeDtypeStruct((bsz, t, A_HEADS * HEAD_DIM), BF16),
        scratch_shapes=[pltpu.VMEM((IDX_HEADS * qb, IDX_DIM), BF16),
                        pltpu.VMEM((IDX_HEADS, qb, 128), F32),
                        pltpu.VMEM((n_tiles, qb, lk), jnp.int32),
                        pltpu.VMEM((A_KV_HEADS, 1, A_GROUP * qb), F32),
                        pltpu.VMEM((A_KV_HEADS, 1, A_GROUP * qb), F32),
                        pltpu.VMEM((A_KV_HEADS, HEAD_DIM, A_GROUP * qb), F32)],
        compiler_params=_params("parallel", "arbitrary"),
        name="dsa_attention",
    )(qi, wi, qt, tiles(ki), tiles(k), vt)


GLA_LEVELS = 6
GLA_ROWS_PER_STEP = 512


def _gla_exponent_matrix():
    c = CHUNK
    a = np.zeros((GLA_LEVELS + 1, c, c), np.float32)
    a[0] = np.tril(np.ones((c, c), np.float32))
    for l in range(GLA_LEVELS):
        b = (c // 2) >> l
        for t in range(c):
            ref = (t // (2 * b)) * 2 * b + b
            if t >= ref:
                a[l + 1, t, ref + 1:t + 1] = 1.0
            else:
                a[l + 1, t, t + 1:ref + 1] = 1.0
    return a.reshape((GLA_LEVELS + 1) * c, c)


def _split3_dot(a_bf16, x):
    hi = x.astype(BF16)
    r1 = x - hi.astype(F32)
    mid = r1.astype(BF16)
    lo = (r1 - mid.astype(F32)).astype(BF16)
    return (jnp.dot(a_bf16, hi, preferred_element_type=F32)
            + jnp.dot(a_bf16, mid, preferred_element_type=F32)
            + jnp.dot(a_bf16, lo, preferred_element_type=F32))


TN_DIMS = (((0,), (0,)), ((), ()))


def _gla_body(q_ref, f_ref, i_ref, g_ref, lb_ref, nw_ref, a_ref, s0_ref, o_ref, sfin_ref, st_ref,
              *, n_chunks):
    r = pl.program_id(2)
    c = CHUNK

    @pl.when(r == 0)
    def _():
        st_ref[...] = s0_ref[0, 0].T

    lb = lb_ref[...]
    nw = nw_ref[...]
    amat = a_ref[...]
    t_row = lax.broadcasted_iota(jnp.int32, (c, B_DK), 0)
    t_idx = lax.broadcasted_iota(jnp.int32, (c, c), 0)
    s_idx = lax.broadcasted_iota(jnp.int32, (c, c), 1)
    upper = [((t_row >> (GLA_LEVELS - 1 - l)) & 1) == 1 for l in range(GLA_LEVELS)]
    pair = [((t_idx >> (GLA_LEVELS - l)) == (s_idx >> (GLA_LEVELS - l)))
            & (((t_idx >> (GLA_LEVELS - 1 - l)) & 1) == 1)
            & (((s_idx >> (GLA_LEVELS - 1 - l)) & 1) == 0) for l in range(GLA_LEVELS)]
    eye = t_idx == s_idx

    cs = range(n_chunks)
    rows = [slice(ci * c, (ci + 1) * c) for ci in cs]
    k = [(1.0 - lb) * jax.nn.sigmoid(-f_ref[rw, :]) for rw in rows]
    logf = [jnp.log1p(-jnp.minimum(x, 1.0 - 1e-6)) for x in k]
    qr = [q_ref[rw, :] for rw in rows]
    q = [x * jax.nn.sigmoid(x) for x in qr]
    d_all = [_split3_dot(amat, x) for x in logf]
    att = [jnp.where(eye, jnp.sum(a * b, axis=1, keepdims=True), 0.0) for a, b in zip(q, k)]
    for l in range(GLA_LEVELS):
        x = [(jnp.where(upper[l], a, b) * jnp.exp(d[(l + 1) * c:(l + 2) * c])).astype(BF16)
             for a, b, d in zip(q, k, d_all)]
        gram = [lax.dot_general(y, y, NT_DIMS, preferred_element_type=F32) for y in x]
        att = [a + jnp.where(pair[l], g, 0.0) for a, g in zip(att, gram)]
    gcum = [d[0:c] for d in d_all]
    g_last = [x[c - 1:c] for x in gcum]
    q_dec = [(a * jnp.exp(gc)).astype(BF16) for a, gc in zip(q, gcum)]
    k_dec = [(a * jnp.exp(gl - gc)).astype(BF16) for a, gl, gc in zip(k, g_last, gcum)]
    vb = [i_ref[rw, :].astype(BF16) for rw in rows]
    o_intra = [jnp.dot(a.astype(BF16), v, preferred_element_type=F32) for a, v in zip(att, vb)]
    kv = [lax.dot_general(v, kd, TN_DIMS, preferred_element_type=F32) for v, kd in zip(vb, k_dec)]
    e_last = [jnp.exp(x) for x in g_last]

    st = st_ref[...]
    for ci in cs:
        o = o_intra[ci] + lax.dot_general(q_dec[ci], st.astype(BF16), NT_DIMS,
                                          preferred_element_type=F32)
        st = st * e_last[ci] + kv[ci]
        o = o * lax.rsqrt(jnp.mean(o * o, axis=1, keepdims=True) + EPS) * nw
        gr = g_ref[rows[ci], :]
        o_ref[rows[ci], :] = (o * (gr * jax.nn.sigmoid(gr))).astype(o_ref.dtype)
    st_ref[...] = st

    @pl.when(r == pl.num_programs(2) - 1)
    def _():
        sfin_ref[0, 0] = st.T


def hgrn2_pallas(proj_b, row0, bsz, t, lb, norm_w, s0):
    rps = min(t, GLA_ROWS_PER_STEP)
    steps = t // rps
    blk0 = row0 // rps
    seg = lambda s: pl.BlockSpec((rps, B_DK), lambda b, h, r: (blk0 + b * steps + r, s * B_HEADS + h))
    state = pl.BlockSpec((1, 1, B_DK, B_DV), lambda b, h, r: (b, h, 0, 0))
    amat = jnp.asarray(_gla_exponent_matrix(), BF16)
    return pl.pallas_call(
        partial(_gla_body, n_chunks=rps // CHUNK),
        grid=(bsz, B_HEADS, steps),
        in_specs=[seg(0), seg(1), seg(2), seg(3),
                  pl.BlockSpec((1, B_DK), lambda b, h, r: (0, h)),
                  pl.BlockSpec((1, B_DV), lambda b, h, r: (0, 0)),
                  pl.BlockSpec(amat.shape, lambda b, h, r: (0, 0)),
                  state],
        out_specs=[pl.BlockSpec((rps, B_DV), lambda b, h, r: (b * steps + r, h)), state],
        out_shape=[jax.ShapeDtypeStruct((bsz * t, B_HEADS * B_DV), BF16),
                   jax.ShapeDtypeStruct((bsz, B_HEADS, B_DK, B_DV), F32)],
        scratch_shapes=[pltpu.VMEM((B_DV, B_DK), F32)],
        compiler_params=_params("parallel", "parallel", "arbitrary"),
        name="hgrn2",
    )(proj_b, proj_b, proj_b, proj_b, lb.reshape(1, -1).astype(F32),
      norm_w.reshape(1, -1).astype(F32), amat, s0)


GDN_PROBLEMS_PER_STEP = 32
GDN_CHUNKS_PER_STEP = 2


def _split3_dot_tn(x, u_bf16):
    hi = x.astype(BF16)
    r1 = x - hi.astype(F32)
    mid = r1.astype(BF16)
    lo = (r1 - mid.astype(F32)).astype(BF16)
    return (lax.dot_general(hi, u_bf16, TN_DIMS, preferred_element_type=F32)
            + lax.dot_general(mid, u_bf16, TN_DIMS, preferred_element_type=F32)
            + lax.dot_general(lo, u_bf16, TN_DIMS, preferred_element_type=F32))


def _gdn_body(q_ref, k_ref, v_ref, z_ref, beta_ref, g_ref, nw_ref, s0_ref, o_ref, sfin_ref, s_ref,
              *, n_chunks, heads):
    hg = pl.program_id(1)
    r = pl.program_id(2)
    c = CHUNK
    rep = C_V_HEADS // C_QK_HEADS

    @pl.when(r == 0)
    def _():
        s_ref[...] = s0_ref[0]

    nw = nw_ref[...]
    t_idx = lax.broadcasted_iota(jnp.int32, (c, c), 0)
    s_idx = lax.broadcasted_iota(jnp.int32, (c, c), 1)
    incl = s_idx <= t_idx
    strict = s_idx < t_idx
    eye = jnp.where(t_idx == s_idx, 1.0, 0.0)
    tril_b = jnp.where(incl, 1.0, 0.0).astype(BF16)
    triu_b = jnp.where(t_idx <= s_idx, 1.0, 0.0).astype(BF16)
    pair = [((t_idx >> (l + 1)) == (s_idx >> (l + 1)))
            & (((t_idx >> l) & 1) == 1) & (((s_idx >> l) & 1) == 0) for l in range(GLA_LEVELS)]
    lane_head = lax.broadcasted_iota(jnp.int32, (c, C_V_HEADS), 1)

    probs = [(j, ci) for j in range(heads) for ci in range(n_chunks)]
    cs = range(len(probs))
    rows = [slice(ci * c, (ci + 1) * c) for _, ci in probs]
    vcol = [slice(j * C_DV, (j + 1) * C_DV) for j, _ in probs]
    qcol = [slice((j // rep) * C_DK, (j // rep + 1) * C_DK) for j, _ in probs]
    head_sel = [lane_head == hg * heads + j for j, _ in probs]
    qa = [q_ref[rw, qc] for rw, qc in zip(rows, qcol)]
    ka = [k_ref[rw, qc] for rw, qc in zip(rows, qcol)]
    q = [x * lax.rsqrt(jnp.sum(x * x, axis=1, keepdims=True) + EPS) * (C_DK ** -0.5) for x in qa]
    k = [x * lax.rsqrt(jnp.sum(x * x, axis=1, keepdims=True) + EPS) for x in ka]
    beta = [jnp.sum(jnp.where(hs, beta_ref[rw, :], 0.0), axis=1, keepdims=True)
            for rw, hs in zip(rows, head_sel)]
    gb = [jnp.broadcast_to(jnp.sum(jnp.where(hs, g_ref[rw, :], 0.0), axis=1, keepdims=True),
                           (c, C_DK)) for rw, hs in zip(rows, head_sel)]
    gcum = [_split3_dot(tril_b, x) for x in gb]
    gc_row = [_split3_dot_tn(x[:, :c], triu_b) for x in gb]
    decay = [jnp.where(incl, jnp.exp(jnp.where(incl, a[:, :c] - b, 0.0)), 0.0)
             for a, b in zip(gcum, gc_row)]
    kb = [a * b for a, b in zip(k, beta)]
    kbf = [x.astype(BF16) for x in k]
    a_mat = [jnp.where(strict, lax.dot_general(a.astype(BF16), b, NT_DIMS,
                                               preferred_element_type=F32) * d, 0.0)
             for a, b, d in zip(kb, kbf, decay)]
    tinv = [eye for _ in cs]
    for l in range(GLA_LEVELS):
        tb = [x.astype(BF16) for x in tinv]
        xt = [jnp.dot(jnp.where(pair[l], a, 0.0).astype(BF16), b, preferred_element_type=F32)
              for a, b in zip(a_mat, tb)]
        tinv = [t0 - jnp.dot(b, x.astype(BF16), preferred_element_type=F32)
                for t0, b, x in zip(tinv, tb, xt)]
    e_cum = [jnp.exp(x) for x in gcum]
    rhs = [jnp.concatenate([v_ref[rw, vc] * bt, b * e], axis=1)
           for rw, vc, bt, b, e in zip(rows, vcol, beta, kb, e_cum)]
    sol = [x + jnp.dot((t0 - eye).astype(BF16), x.astype(BF16), preferred_element_type=F32)
           for x, t0 in zip(rhs, tinv)]
    qk = [jnp.where(incl, lax.dot_general(a.astype(BF16), b, NT_DIMS,
                                          preferred_element_type=F32) * d, 0.0).astype(BF16)
          for a, b, d in zip(q, kbf, decay)]
    g_last = [x[c - 1:c] for x in gcum]
    q_dec = [(a * e).astype(BF16) for a, e in zip(q, e_cum)]
    k_dec = [(a * jnp.exp(gl - gc)).astype(BF16) for a, gl, gc in zip(k, g_last, gcum)]

    s = [s_ref[j] for j in range(heads)]
    for ci in range(n_chunks):
        ps = [j * n_chunks + ci for j in range(heads)]
        sb = [x.astype(BF16) for x in s]
        ub = [(sol[p][:, :C_DV] - jnp.dot(sol[p][:, C_DV:].astype(BF16), b,
                                          preferred_element_type=F32)).astype(BF16)
              for p, b in zip(ps, sb)]
        o = [jnp.dot(q_dec[p], b, preferred_element_type=F32)
             + jnp.dot(qk[p], u, preferred_element_type=F32) for p, b, u in zip(ps, sb, ub)]
        s = [jnp.exp(g_last[p]) * x + lax.dot_general(k_dec[p], u, TN_DIMS, preferred_element_type=F32)
             for p, x, u in zip(ps, s, ub)]
        for p, x in zip(ps, o):
            x = x * lax.rsqrt(jnp.mean(x * x, axis=1, keepdims=True) + EPS) * nw
            z = z_ref[rows[p], vcol[p]]
            o_ref[rows[p], vcol[p]] = (x * (z * jax.nn.sigmoid(z))).astype(o_ref.dtype)
    for j in range(heads):
        s_ref[j] = s[j]

    @pl.when(r == pl.num_programs(2) - 1)
    def _():
        for j in range(heads):
            sfin_ref[0, j] = s[j]


def gdn_pallas(act, proj_m, beta, g, row0, bsz, t, norm_w, s0):
    n_chunks = min(t // CHUNK, GDN_CHUNKS_PER_STEP)
    heads = GDN_PROBLEMS_PER_STEP // n_chunks
    rps = n_chunks * CHUNK
    steps = t // rps
    blk0 = row0 // rps
    rep = C_V_HEADS // C_QK_HEADS
    qk_w = (heads // rep) * C_DK
    v_w = heads * C_DV
    qk_blocks = C_QK_HEADS * C_DK // qk_w
    qk_spec = lambda seg: pl.BlockSpec((rps, qk_w), lambda b, h, r: (b * steps + r, seg * qk_blocks + h))
    small = pl.BlockSpec((rps, C_V_HEADS), lambda b, h, r: (b * steps + r, 0))
    state = pl.BlockSpec((1, heads, C_DK, C_DV), lambda b, h, r: (b, h, 0, 0))
    return pl.pallas_call(
        partial(_gdn_body, n_chunks=n_chunks, heads=heads),
        grid=(bsz, C_V_HEADS // heads, steps),
        in_specs=[qk_spec(0), qk_spec(1),
                  pl.BlockSpec((rps, v_w), lambda b, h, r: (b * steps + r, 2 * C_QK_HEADS * C_DK // v_w + h)),
                  pl.BlockSpec((rps, v_w), lambda b, h, r: (blk0 + b * steps + r, C_CONV_CH // v_w + h)),
                  small, small,
                  pl.BlockSpec((1, C_DV), lambda b, h, r: (0, 0)),
                  state],
        out_specs=[pl.BlockSpec((rps, v_w), lambda b, h, r: (b * steps + r, h)), state],
        out_shape=[jax.ShapeDtypeStruct((bsz * t, C_V_HEADS * C_DV), BF16),
                   jax.ShapeDtypeStruct((bsz, C_V_HEADS, C_DK, C_DV), F32)],
        scratch_shapes=[pltpu.VMEM((heads, C_DK, C_DV), F32)],
        compiler_params=_params("parallel", "parallel", "arbitrary"),
        name="gated_deltanet",
    )(act, act, act, proj_m, beta, g, norm_w.reshape(1, -1).astype(F32), s0)


EVEN_A_COLS = 7680
OFF_AK, OFF_AV, OFF_IQ, OFF_IK, OFF_IW = 2048, 2560, 3072, 7168, 7296


def _rope_tables(pos):
    posf = pos.astype(F32)[:, None]

    def cs(half):
        inv_freq = ROPE_THETA ** (-jnp.arange(half, dtype=F32) / half)
        ang = posf * inv_freq[None, :]
        return jnp.cos(ang), jnp.sin(ang)

    c64, s64 = cs(HEAD_DIM // 2)
    c32, s32 = cs(IDX_ROPE_DIM // 2)
    z32 = jnp.zeros_like(s32)
    rest = IDX_DIM - IDX_ROPE_DIM
    ones = jnp.ones((pos.shape[0], rest), F32)
    zeros = jnp.zeros((pos.shape[0], rest), F32)
    return (jnp.concatenate([c64, c64], 1), jnp.concatenate([-s64, s64], 1),
            jnp.concatenate([c32, c32, ones], 1), jnp.concatenate([-s32, z32, zeros], 1),
            jnp.concatenate([z32, s32, zeros], 1))


def _even_prep_body(p_ref, ca_ref, sa_ref, ci_ref, s1_ref, s2_ref, qn_ref, kn_ref,
                    q_ref, k_ref, kb_ref, v_ref, vb_ref, iq_ref, ik_ref, ikb_ref, wi_ref):
    ca, sa = ca_ref[...], sa_ref[...]
    ci, s1, s2 = ci_ref[...], s1_ref[...], s2_ref[...]

    def head(off, j):
        return p_ref[:, off + j * HEAD_DIM:off + (j + 1) * HEAD_DIM]

    def norm_rope(x, w):
        y = x * lax.rsqrt(jnp.mean(x * x, axis=1, keepdims=True) + EPS) * w
        return y * ca + pltpu.roll(y, HEAD_DIM // 2, 1) * sa

    def idx_rope(x):
        half = IDX_ROPE_DIM // 2
        return x * ci + pltpu.roll(x, IDX_DIM - half, 1) * s1 + pltpu.roll(x, half, 1) * s2

    qn, kn = qn_ref[...], kn_ref[...]
    for j in range(A_HEADS):
        cols = slice(j * HEAD_DIM, (j + 1) * HEAD_DIM)
        q_ref[:, cols] = (norm_rope(head(0, j), qn) * (HEAD_DIM ** -0.5)).astype(BF16)
    for j in range(A_KV_HEADS):
        cols = slice(j * HEAD_DIM, (j + 1) * HEAD_DIM)
        kk = norm_rope(head(OFF_AK, j), kn)
        k_ref[:, cols] = kk
        kb_ref[:, cols] = kk.astype(BF16)
        vv = head(OFF_AV, j)
        v_ref[:, cols] = vv
        vb_ref[:, cols] = vv.astype(BF16)
    for j in range(IDX_HEADS):
        cols = slice(j * IDX_DIM, (j + 1) * IDX_DIM)
        iq_ref[:, cols] = idx_rope(head(OFF_IQ, j)).astype(BF16)
    ik = idx_rope(head(OFF_IK, 0))
    ik_ref[...] = ik
    ikb_ref[...] = ik.astype(BF16)
    wi_ref[...] = p_ref[:, OFF_IW:OFF_IW + IDX_HEADS] * IDX_SCALE


def even_prep(proj_a, pos, q_norm, k_norm, tm=256):
    m = proj_a.shape[0]
    tabs = _rope_tables(pos)
    row = lambda w: pl.BlockSpec((tm, w), lambda i: (i, 0))
    vec = pl.BlockSpec((1, HEAD_DIM), lambda i: (0, 0))
    kvw = A_KV_HEADS * HEAD_DIM
    shapes = [(A_HEADS * HEAD_DIM, BF16), (kvw, F32), (kvw, BF16), (kvw, F32), (kvw, BF16),
              (IDX_HEADS * IDX_DIM, BF16), (IDX_DIM, F32), (IDX_DIM, BF16), (IDX_HEADS, F32)]
    return pl.pallas_call(
        _even_prep_body,
        grid=(m // tm,),
        in_specs=[row(EVEN_A_COLS)] + [row(HEAD_DIM)] * 5 + [vec, vec],
        out_specs=[row(w) for w, _ in shapes],
        out_shape=[jax.ShapeDtypeStruct((m, w), dt) for w, dt in shapes],
        compiler_params=_params("parallel"),
        name="even_prep",
    )(proj_a, *tabs, q_norm.reshape(1, -1).astype(F32), k_norm.reshape(1, -1).astype(F32))


def _even_mixer(prep, proj_b, row0, bsz, t, past, lb, bnorm):
    m = bsz * t
    q_bf, k_f, k_bf, v_f, v_bf, iq_bf, ik_f, ik_bf, wi = [a[row0:row0 + m].reshape(bsz, t, -1) for a in prep]
    past_len = 0 if past is None else past[0].shape[1]
    if past is None:
        keys, vals, ikeys = k_bf, v_bf, ik_bf
        s0_b = jnp.zeros((bsz, B_HEADS, B_DK, B_DV), F32)
    else:
        flat = lambda a: a.reshape(bsz, past_len, -1).astype(BF16)
        keys = jnp.concatenate([flat(past[0]), k_bf], axis=1)
        vals = jnp.concatenate([flat(past[1]), v_bf], axis=1)
        ikeys = jnp.concatenate([flat(past[2]), ik_bf], axis=1)
        s0_b = past[3]
    o_a = dsa_attention_pallas(q_bf, keys, vals, iq_bf, ikeys, wi, past_len)
    o_b, s_b = hgrn2_pallas(proj_b, row0, bsz, t, lb, bnorm, s0_b)
    mixed_in = jnp.concatenate([o_a.reshape(m, -1), o_b], axis=-1)
    new_k = k_f.reshape(bsz, t, A_KV_HEADS, HEAD_DIM)
    new_v = v_f.reshape(bsz, t, A_KV_HEADS, HEAD_DIM)
    return mixed_in, (new_k, new_v, ik_f, s_b)


CONV_HALO = 8


def _conv_body(x_ref, st_ref, w_ref, o_ref, carry_ref):
    r = pl.program_id(2)
    x = x_ref[...]
    tm = x.shape[0]
    prev = jnp.where(r == 0, st_ref[0], carry_ref[...])
    top = x[0:CONV_HALO]
    row = lax.broadcasted_iota(jnp.int32, top.shape, 0)
    acc = x * w_ref[C_CONV - 1:C_CONV]
    acc_top = top * w_ref[C_CONV - 1:C_CONV]
    for k in range(1, C_CONV):
        w = w_ref[C_CONV - 1 - k:C_CONV - k]
        acc = acc + pltpu.roll(x, k, 0) * w
        acc_top = acc_top + jnp.where(row >= k, pltpu.roll(top, k, 0), pltpu.roll(prev, k, 0)) * w
    o_ref[...] = acc * jax.nn.sigmoid(acc)
    o_ref[0:CONV_HALO] = acc_top * jax.nn.sigmoid(acc_top)
    carry_ref[...] = x[tm - CONV_HALO:tm]


def conv_silu(proj_m, row0, bsz, t, conv_state, conv_w, tc=1024):
    tm = min(t, 512)
    steps = t // tm
    blk0 = row0 // tm
    st8 = jnp.pad(conv_state, ((0, 0), (CONV_HALO - (C_CONV - 1), 0), (0, 0)))
    return pl.pallas_call(
        _conv_body,
        grid=(C_CONV_CH // tc, bsz, steps),
        in_specs=[pl.BlockSpec((tm, tc), lambda c, b, r: (blk0 + b * steps + r, c)),
                  pl.BlockSpec((1, CONV_HALO, tc), lambda c, b, r: (b, 0, c)),
                  pl.BlockSpec((C_CONV, tc), lambda c, b, r: (0, c))],
        out_specs=pl.BlockSpec((tm, tc), lambda c, b, r: (b * steps + r, c)),
        out_shape=jax.ShapeDtypeStruct((bsz * t, C_CONV_CH), F32),
        scratch_shapes=[pltpu.VMEM((CONV_HALO, tc), F32)],
        compiler_params=_params("parallel", "parallel", "arbitrary"),
        name="conv_silu",
    )(proj_m, st8, conv_w.astype(F32))


def _odd_mixer(proj_m, proj_s, row0, bsz, t, past, conv_w, a_log, dt_bias, onorm):
    m = bsz * t
    ps = proj_s[row0:row0 + m]
    b_raw, a_raw = ps[:, :C_V_HEADS], ps[:, C_V_HEADS:2 * C_V_HEADS]
    if past is None:
        conv_state = jnp.zeros((bsz, C_CONV - 1, C_CONV_CH), F32)
        s0_c = jnp.zeros((bsz, C_V_HEADS, C_DK, C_DV), F32)
    else:
        s0_c, conv_state = past
    act = conv_silu(proj_m, row0, bsz, t, conv_state, conv_w)
    tail = [proj_m[row0 + b * t + t - (C_CONV - 1):row0 + (b + 1) * t, :C_CONV_CH] for b in range(bsz)]
    new_conv = jnp.stack(tail)
    beta = jax.nn.sigmoid(b_raw)
    g = -jnp.exp(a_log.astype(F32)) * jax.nn.softplus(a_raw + dt_bias.astype(F32))
    o_c, s_c = gdn_pallas(act, proj_m, beta, g, row0, bsz, t, onorm, s0_c)
    return o_c, (s_c, new_conv)


def _pad_cols(w, n):
    return jnp.pad(w, ((0, 0), (0, n - w.shape[1])))


def kernel(x_prompt, x_sample, cache_a_k, cache_a_v, cache_a_kidx, state_b, state_c, state_c_conv, p_prompt, p_sample, ffn1_norm, ffn1_w_gate, ffn1_w_up, ffn1_w_down, mix_norm, even_w_in, even_w_out, a_q_norm, a_k_norm, b_lb_logits, b_out_norm, odd_w_in, odd_w_out, c_conv_w, c_a_log, c_dt_bias, c_out_norm, ffn2_norm, ffn2_w_gate, ffn2_w_up, ffn2_w_down, ple_norm, ple_w_gate, ple_w_proj, ple_post_norm):
    d = D_MODEL
    bp, tp = x_prompt.shape[:2]
    bs, ts = x_sample.shape[:2]
    mp, ms = bp * tp, bs * ts
    x = jnp.concatenate([x_prompt.reshape(mp, d), x_sample.reshape(ms, d)], axis=0)
    p_all = jnp.concatenate([p_prompt.reshape(DEPTH, mp, -1), p_sample.reshape(DEPTH, ms, -1)],
                            axis=1).astype(BF16)

    past_len = cache_a_k.shape[2]
    pos_all = jnp.concatenate([jnp.tile(jnp.arange(tp, dtype=jnp.int32), bp),
                               jnp.tile(past_len + jnp.arange(ts, dtype=jnp.int32), bs)])

    lb_soft = jax.nn.softmax(b_lb_logits.astype(F32), axis=0)
    lbs = jnp.cumsum(lb_soft, axis=0) - lb_soft[0]

    new_p = [[] for _ in range(6)]
    new_s = [[] for _ in range(6)]
    for layer in range(DEPTH):
        j = layer // 2
        x = ffn(x, ffn1_norm[layer], ffn1_w_gate[layer].astype(BF16),
                ffn1_w_up[layer].astype(BF16), ffn1_w_down[layer].astype(BF16))
        h = rmsnorm(x, mix_norm[layer])
        if layer % 2 == 0:
            w_in = even_w_in[j]
            w_a = _pad_cols(w_in[:, :7328], 7680).astype(BF16)
            w_b = w_in[:, 7328:].astype(BF16)
            proj_a = matmul(h, w_a, name="even_in_a")
            proj_b = matmul(h, w_b, name="even_in_b")
            prep = even_prep(proj_a, pos_all, a_q_norm[j], a_k_norm[j])
            mix_p, st_p = _even_mixer(prep, proj_b, 0, bp, tp, None, lbs[j], b_out_norm[j])
            mix_s, st_s = _even_mixer(prep, proj_b, mp, bs, ts,
                                      (cache_a_k[j], cache_a_v[j], cache_a_kidx[j], state_b[j]),
                                      lbs[j], b_out_norm[j])
            for idx in range(4):
                new_p[idx].append(st_p[idx])
                new_s[idx].append(st_s[idx])
            w_out = even_w_out[j].astype(BF16)
        else:
            w_in = odd_w_in[j]
            w_m = w_in[:, :12288].astype(BF16)
            w_s = _pad_cols(w_in[:, 12288:], 128).astype(BF16)
            proj_m = matmul(h, w_m, name="odd_in_m")
            proj_s = matmul(h, w_s, tn=128, name="odd_in_s")
            mix_p, st_p = _odd_mixer(proj_m, proj_s, 0, bp, tp, None,
                                     c_conv_w[j], c_a_log[j], c_dt_bias[j], c_out_norm[j])
            mix_s, st_s = _odd_mixer(proj_m, proj_s, mp, bs, ts,
                                     (state_c[j], state_c_conv[j]),
                                     c_conv_w[j], c_a_log[j], c_dt_bias[j], c_out_norm[j])
            for idx in range(2):
                new_p[4 + idx].append(st_p[idx])
                new_s[4 + idx].append(st_s[idx])
            w_out = odd_w_out[j].astype(BF16)
        mix_in = jnp.concatenate([mix_p, mix_s], axis=0).astype(BF16)
        x = matmul(mix_in, w_out, x, body=_mm_resid_body, name="mix_out")
        x = ffn(x, ffn2_norm[layer], ffn2_w_gate[layer].astype(BF16),
                ffn2_w_up[layer].astype(BF16), ffn2_w_down[layer].astype(BF16))
        hp = rmsnorm(x, ple_norm[layer])
        pp = ple_proj(p_all[layer], ple_w_proj[layer].astype(BF16), ple_post_norm[layer])
        x = matmul(hp, ple_w_gate[layer].astype(BF16), x, pp, body=_mm_gate_body, name="ple_gate")

    y_prompt = x[:mp].reshape(bp, tp, d)
    y_sample = x[mp:].reshape(bs, ts, d)
    outs_p = tuple(jnp.stack(v) for v in new_p)
    outs_s = tuple(jnp.stack(v) for v in new_s)
    return (y_prompt, y_sample) + outs_p + outs_s
```

```python
from functools import partial

import numpy as np

import jax
import jax.numpy as jnp
from jax import lax
from jax.experimental import pallas as pl
from jax.experimental.pallas import tpu as pltpu

D_MODEL = 4096
DEPTH = 4
CHUNK = 64
QBLOCK = 128
EPS = 1e-6
NEG_BIG = -1e30
ROPE_THETA = 10000.0
HEAD_DIM = 128
A_HEADS = 16
A_KV_HEADS = 4
IDX_HEADS = 32
IDX_DIM = 128
IDX_ROPE_DIM = 64
TOPK_MAX = 256
IDX_SCALE = (IDX_HEADS * IDX_DIM) ** -0.5
B_HEADS = 16
B_DK = 128
B_DV = 128
C_QK_HEADS = 16
C_V_HEADS = 32
C_DK = 128
C_DV = 128
C_CONV = 4
C_CONV_CH = 2 * C_QK_HEADS * C_DK + C_V_HEADS * C_DV

V7X_VMEM_LIMIT_BYTES = 56 * 1024 * 1024

BF16 = jnp.bfloat16
F32 = jnp.float32


def _params(*sem):
    return pltpu.CompilerParams(dimension_semantics=sem,
                                vmem_limit_bytes=V7X_VMEM_LIMIT_BYTES)


def _rmsnorm_body(x_ref, w_ref, o_ref):
    x = x_ref[...]
    ms = jnp.mean(x * x, axis=-1, keepdims=True)
    o_ref[...] = (x * lax.rsqrt(ms + EPS) * w_ref[...]).astype(o_ref.dtype)


def rmsnorm(x, w, tm=512, out_dtype=BF16):
    m, d = x.shape
    return pl.pallas_call(
        _rmsnorm_body,
        grid=(m // tm,),
        in_specs=[pl.BlockSpec((tm, d), lambda i: (i, 0)),
                  pl.BlockSpec((1, d), lambda i: (0, 0))],
        out_specs=pl.BlockSpec((tm, d), lambda i: (i, 0)),
        out_shape=jax.ShapeDtypeStruct((m, d), out_dtype),
        compiler_params=_params("parallel"),
        name="rmsnorm",
    )(x, w.reshape(1, d).astype(F32))


def _mm_body(a_ref, w_ref, o_ref):
    o_ref[...] = jnp.dot(a_ref[...], w_ref[...], preferred_element_type=F32)


def _mm_resid_body(a_ref, w_ref, r_ref, o_ref):
    o_ref[...] = r_ref[...] + jnp.dot(a_ref[...], w_ref[...], preferred_element_type=F32)


def _mm_gate_body(a_ref, w_ref, r_ref, p_ref, o_ref):
    acc = jnp.dot(a_ref[...], w_ref[...], preferred_element_type=F32)
    o_ref[...] = r_ref[...] + jax.nn.sigmoid(acc) * p_ref[...]


def matmul(a, w, *extras, tm=1024, tn=512, body=_mm_body, name="matmul"):
    m, k = a.shape
    n = w.shape[1]
    tile = pl.BlockSpec((tm, tn), lambda i, j: (i, j))
    return pl.pallas_call(
        body,
        grid=(m // tm, n // tn),
        in_specs=[pl.BlockSpec((tm, k), lambda i, j: (i, 0)),
                  pl.BlockSpec((k, tn), lambda i, j: (0, j))] + [tile] * len(extras),
        out_specs=tile,
        out_shape=jax.ShapeDtypeStruct((m, n), F32),
        compiler_params=_params("parallel", "arbitrary"),
        name=name,
    )(a, w, *extras)


def _ple_proj_body(p_ref, w_ref, nw_ref, o_ref):
    y = jnp.dot(p_ref[...], w_ref[...], preferred_element_type=F32)
    ms = jnp.mean(y * y, axis=-1, keepdims=True)
    o_ref[...] = y * lax.rsqrt(ms + EPS) * nw_ref[...]


def ple_proj(p, w, nw, tm=512):
    m, k = p.shape
    n = w.shape[1]
    return pl.pallas_call(
        _ple_proj_body,
        grid=(m // tm,),
        in_specs=[pl.BlockSpec((tm, k), lambda i: (i, 0)),
                  pl.BlockSpec((k, n), lambda i: (0, 0)),
                  pl.BlockSpec((1, n), lambda i: (0, 0))],
        out_specs=pl.BlockSpec((tm, n), lambda i: (i, 0)),
        out_shape=jax.ShapeDtypeStruct((m, n), F32),
        compiler_params=_params("parallel"),
        name="ple_proj",
    )(p, w, nw.reshape(1, n).astype(F32))


FFN_NORM_ROWS = 64


def _ffn_body(x_ref, nw_ref, wg_ref, wu_ref, wd_ref, o_ref, h_ref):
    f = pl.program_id(1)

    @pl.when(f == 0)
    def _():
        for r0 in range(0, x_ref.shape[0], FFN_NORM_ROWS):
            rows = slice(r0, r0 + FFN_NORM_ROWS)
            x = x_ref[rows, :]
            ms = jnp.mean(x * x, axis=-1, keepdims=True)
            h_ref[rows, :] = (x * lax.rsqrt(ms + EPS) * nw_ref[...]).astype(BF16)
        o_ref[...] = jnp.zeros_like(o_ref)

    h = h_ref[...]
    g = jnp.dot(h, wg_ref[...], preferred_element_type=F32)
    u = jnp.dot(h, wu_ref[...], preferred_element_type=F32)
    act = (g * jax.nn.sigmoid(g) * u).astype(BF16)
    o_ref[...] += jnp.dot(act, wd_ref[...], preferred_element_type=F32)

    @pl.when(f == pl.num_programs(1) - 1)
    def _():
        o_ref[...] = x_ref[...] + 0.5 * o_ref[...]


def ffn(x, nw, wg, wu, wd, tm=768, tf=256):
    m, d = x.shape
    dff = wg.shape[1]
    return pl.pallas_call(
        _ffn_body,
        grid=(m // tm, dff // tf),
        in_specs=[pl.BlockSpec((tm, d), lambda i, f: (i, 0), pipeline_mode=pl.Buffered(1)),
                  pl.BlockSpec((1, d), lambda i, f: (0, 0)),
                  pl.BlockSpec((d, tf), lambda i, f: (0, f)),
                  pl.BlockSpec((d, tf), lambda i, f: (0, f)),
                  pl.BlockSpec((tf, d), lambda i, f: (f, 0))],
        out_specs=pl.BlockSpec((tm, d), lambda i, f: (i, 0)),
        out_shape=jax.ShapeDtypeStruct((m, d), F32),
        scratch_shapes=[pltpu.VMEM((tm, d), BF16)],
        compiler_params=_params("parallel", "arbitrary"),
        name="ffn",
    )(x, nw.reshape(1, d).astype(F32), wg, wu, wd)


DSA_KEY_TILE = 512
DSA_HEAD_GROUP = 8
DSA_HEADS_INTERLEAVED = 4
INT32_MIN = -2 ** 31
A_GROUP = A_HEADS // A_KV_HEADS
NT_DIMS = (((1,), (1,)), ((), ()))


def _dsa_body(qi_ref, wi_ref, qt_ref, ki_ref, k_ref, vt_ref, o_ref,
              qi_st, wib, skey, m_s, l_s, acc_s, *, qb, past_len, l_valid, topk, idx_bits):
    lk = DSA_KEY_TILE
    i = pl.program_id(1)
    q_pos0 = past_len + i * qb
    limit = jnp.minimum(((q_pos0 + qb - 1) // CHUNK + 1) * CHUNK, l_valid)
    nt = (limit + lk - 1) // lk

    for h in range(IDX_HEADS):
        qi_st[h * qb:(h + 1) * qb, :] = qi_ref[0, :, h * IDX_DIM:(h + 1) * IDX_DIM]
        wib[h] = jnp.broadcast_to(wi_ref[0, :, h:h + 1], (qb, 128))

    q_chunk = (q_pos0 + lax.broadcasted_iota(jnp.int32, (qb, 128), 0)) // CHUNK

    def score_tile(t, carry):
        kt = ki_ref[0, t]
        cols = [jnp.zeros((qb, 128), F32) for _ in range(lk // 128)]
        for hg in range(IDX_HEADS // DSA_HEAD_GROUP):
            rows = DSA_HEAD_GROUP * qb
            lg = lax.dot_general(qi_st[hg * rows:(hg + 1) * rows, :], kt, NT_DIMS,
                                 preferred_element_type=F32)
            for hh in range(DSA_HEAD_GROUP):
                w = wib[hg * DSA_HEAD_GROUP + hh]
                for c in range(lk // 128):
                    blk = lg[hh * qb:(hh + 1) * qb, c * 128:(c + 1) * 128]
                    cols[c] = cols[c] + jnp.maximum(blk, 0.0) * w
        for c in range(lk // 128):
            k_pos = t * lk + c * 128 + lax.broadcasted_iota(jnp.int32, (qb, 128), 1)
            adm = (k_pos // CHUNK <= q_chunk) & (k_pos < l_valid)
            bits = lax.bitcast_convert_type(cols[c], jnp.int32)
            key = jnp.where(bits < 0, bits ^ jnp.int32(0x7FFFFFFF), bits)
            skey[t, :, c * 128:(c + 1) * 128] = jnp.where(adm, key, jnp.int32(INT32_MIN))
        return carry

    lax.fori_loop(0, nt, score_tile, 0)

    def count_ge(cand):
        def body(t, acc):
            for c in range(lk // 128):
                acc = acc + jnp.where(skey[t, :, c * 128:(c + 1) * 128] >= cand, 1.0, 0.0)
            return acc
        acc = lax.fori_loop(0, nt, body, jnp.zeros((qb, 128), F32))
        return jnp.sum(acc, axis=1, keepdims=True)

    kf = jnp.float32(topk)
    zero = jnp.zeros((qb, 128), jnp.int32)
    prefix = jnp.where(count_ge(zero) >= kf, zero, jnp.int32(INT32_MIN))

    def bit_step(s, prefix):
        cand = prefix + lax.shift_left(jnp.int32(1), jnp.int32(30) - s)
        return jnp.where(count_ge(cand) >= kf, cand, prefix)

    prefix = lax.fori_loop(0, 31, bit_step, prefix)
    thr = jnp.maximum(prefix, jnp.int32(INT32_MIN + 1))

    real = prefix > jnp.int32(INT32_MIN)
    tie_rows = jnp.where((count_ge(prefix) > kf) & real[:, 0:1], 1.0, 0.0)

    @pl.when(jnp.max(tie_rows) > 0.0)
    def _():
        lane = lax.broadcasted_iota(jnp.int32, (qb, 128), 1)

        def count_tiles(pred):
            def body(t, acc):
                for c in range(lk // 128):
                    tile = skey[t, :, c * 128:(c + 1) * 128]
                    acc = acc + jnp.where(pred(tile, t * lk + c * 128 + lane), 1.0, 0.0)
                return acc
            acc = lax.fori_loop(0, nt, body, jnp.zeros((qb, 128), F32))
            return jnp.sum(acc, axis=1, keepdims=True)

        need = kf - count_tiles(lambda tile, kidx: tile > prefix)

        def cut_step(s, cut):
            cand = cut + lax.shift_left(jnp.int32(1), jnp.int32(idx_bits - 1) - s)
            before = count_tiles(lambda tile, kidx: (tile == prefix) & (kidx < cand))
            return jnp.where(before < need, cand, cut)

        cut = lax.fori_loop(0, idx_bits, cut_step, jnp.zeros((qb, 128), jnp.int32))

        def demote(t, carry):
            for c in range(lk // 128):
                cols = slice(c * 128, (c + 1) * 128)
                tile = skey[t, :, cols]
                drop = real & (tile == prefix) & (t * lk + c * 128 + lane > cut)
                skey[t, :, cols] = jnp.where(drop, prefix - 1, tile)
            return carry

        lax.fori_loop(0, nt, demote, 0)

    m_s[...] = jnp.full(m_s.shape, NEG_BIG, F32)
    l_s[...] = jnp.zeros(l_s.shape, F32)
    acc_s[...] = jnp.zeros(acc_s.shape, F32)

    eye_q = jnp.where(lax.broadcasted_iota(jnp.int32, (qb, qb), 0)
                      == lax.broadcasted_iota(jnp.int32, (qb, qb), 1), 1.0, 0.0).astype(BF16)
    eye_d = jnp.where(lax.broadcasted_iota(jnp.int32, (HEAD_DIM, HEAD_DIM), 0)
                      == lax.broadcasted_iota(jnp.int32, (HEAD_DIM, HEAD_DIM), 1), 1.0, 0.0).astype(BF16)

    def attn_tile(t, carry):
        sel = jnp.concatenate(
            [jnp.where(skey[t, :, c * 128:(c + 1) * 128] >= thr, 1.0, 0.0)
             for c in range(lk // 128)], axis=1).astype(BF16)
        sel_t = lax.dot_general(sel, eye_q, TN_DIMS, preferred_element_type=F32)
        bias = (sel_t - 1.0) * (-NEG_BIG)
        bias4 = jnp.concatenate([bias] * A_GROUP, axis=1)
        for n0 in range(0, A_KV_HEADS, DSA_HEADS_INTERLEAVED):
            ns = range(n0, n0 + DSA_HEADS_INTERLEAVED)
            s = [jnp.dot(k_ref[0, t, :, n * HEAD_DIM:(n + 1) * HEAD_DIM], qt_ref[0, 0, n],
                         preferred_element_type=F32) + bias4 for n in ns]
            m_old = [m_s[n] for n in ns]
            m_new = [jnp.maximum(mo, jnp.max(x, axis=0, keepdims=True)) for mo, x in zip(m_old, s)]
            alpha = [jnp.exp(mo - mn) for mo, mn in zip(m_old, m_new)]
            p = [jnp.exp(x - mn) for x, mn in zip(s, m_new)]
            pv = [jnp.dot(vt_ref[0, t, n], x.astype(BF16), preferred_element_type=F32)
                  for n, x in zip(ns, p)]
            for j, n in enumerate(ns):
                l_s[n] = alpha[j] * l_s[n] + jnp.sum(p[j], axis=0, keepdims=True)
                acc_s[n] = alpha[j] * acc_s[n] + pv[j]
                m_s[n] = m_new[j]
        return carry

    lax.fori_loop(0, nt, attn_tile, 0)

    for n in range(A_KV_HEADS):
        o = (acc_s[n] / l_s[n]).astype(BF16)
        for g in range(A_GROUP):
            hd = n * A_GROUP + g
            o_t = lax.dot_general(o[:, g * qb:(g + 1) * qb], eye_d, TN_DIMS,
                                  preferred_element_type=F32)
            o_ref[0, :, hd * HEAD_DIM:(hd + 1) * HEAD_DIM] = o_t.astype(o_ref.dtype)


def dsa_attention_pallas(q, k, v, qi, ki, wi, past_len):
    bsz, t, _ = q.shape
    l = k.shape[1]
    lk = DSA_KEY_TILE
    topk = min(TOPK_MAX, l // 4)
    qb = min(QBLOCK, t)
    n_tiles = -(-l // lk)
    pad = n_tiles * lk - l

    def tiles(a):
        a = jnp.pad(a, ((0, 0), (0, pad), (0, 0))).astype(BF16)
        return a.reshape(bsz, n_tiles, lk, a.shape[-1])

    nb = t // qb
    qs = q.reshape(bsz, nb, qb, A_KV_HEADS, A_GROUP, HEAD_DIM)
    qt = jnp.transpose(qs, (0, 1, 3, 5, 4, 2)).reshape(bsz, nb, A_KV_HEADS, HEAD_DIM, A_GROUP * qb)
    vt = jnp.transpose(tiles(v).reshape(bsz, n_tiles, lk, A_KV_HEADS, HEAD_DIM), (0, 1, 3, 4, 2))
    whole = lambda shape: pl.BlockSpec(shape, lambda b, i: (b,) + (0,) * (len(shape) - 1),
                                       pipeline_mode=pl.Buffered(1))
    rows = lambda width: pl.BlockSpec((1, qb, width), lambda b, i: (b, i, 0))
    body = partial(_dsa_body, qb=qb, past_len=past_len, l_valid=l, topk=topk,
                   idx_bits=(n_tiles * lk - 1).bit_length())
    return pl.pallas_call(
        body,
        grid=(bsz, nb),
        in_specs=[rows(IDX_HEADS * IDX_DIM), rows(IDX_HEADS),
                  pl.BlockSpec((1, 1, A_KV_HEADS, HEAD_DIM, A_GROUP * qb), lambda b, i: (b, i, 0, 0, 0)),
                  whole((1, n_tiles, lk, IDX_DIM)),
                  whole((1, n_tiles, lk, A_KV_HEADS * HEAD_DIM)),
                  whole((1, n_tiles, A_KV_HEADS, HEAD_DIM, lk))],
        out_specs=rows(A_HEADS * HEAD_DIM),
        out_shape=jax.ShapeDtypeStruct((bsz, t, A_HEADS * HEAD_DIM), BF16),
        scratch_shapes=[pltpu.VMEM((IDX_HEADS * qb, IDX_DIM), BF16),
                        pltpu.VMEM((IDX_HEADS, qb, 128), F32),
                        pltpu.VMEM((n_tiles, qb, lk), jnp.int32),
                        pltpu.VMEM((A_KV_HEADS, 1, A_GROUP * qb), F32),
                        pltpu.VMEM((A_KV_HEADS, 1, A_GROUP * qb), F32),
                        pltpu.VMEM((A_KV_HEADS, HEAD_DIM, A_GROUP * qb), F32)],
        compiler_params=_params("parallel", "arbitrary"),
        name="dsa_attention",
    )(qi, wi, qt, tiles(ki), tiles(k), vt)


GLA_LEVELS = 6
GLA_ROWS_PER_STEP = 512


def _gla_exponent_matrix():
    c = CHUNK
    a = np.zeros((GLA_LEVELS + 1, c, c), np.float32)
    a[0] = np.tril(np.ones((c, c), np.float32))
    for l in range(GLA_LEVELS):
        b = (c // 2) >> l
        for t in range(c):
            ref = (t // (2 * b)) * 2 * b + b
            if t >= ref:
                a[l + 1, t, ref + 1:t + 1] = 1.0
            else:
                a[l + 1, t, t + 1:ref + 1] = 1.0
    return a.reshape((GLA_LEVELS + 1) * c, c)


def _split3_dot(a_bf16, x):
    hi = x.astype(BF16)
    r1 = x - hi.astype(F32)
    mid = r1.astype(BF16)
    lo = (r1 - mid.astype(F32)).astype(BF16)
    return (jnp.dot(a_bf16, hi, preferred_element_type=F32)
            + jnp.dot(a_bf16, mid, preferred_element_type=F32)
            + jnp.dot(a_bf16, lo, preferred_element_type=F32))


TN_DIMS = (((0,), (0,)), ((), ()))


def _gla_body(q_ref, f_ref, i_ref, g_ref, lb_ref, nw_ref, a_ref, s0_ref, o_ref, sfin_ref, st_ref,
              *, n_chunks):
    r = pl.program_id(2)
    c = CHUNK

    @pl.when(r == 0)
    def _():
        st_ref[...] = s0_ref[0, 0].T

    lb = lb_ref[...]
    nw = nw_ref[...]
    amat = a_ref[...]
    t_row = lax.broadcasted_iota(jnp.int32, (c, B_DK), 0)
    t_idx = lax.broadcasted_iota(jnp.int32, (c, c), 0)
    s_idx = lax.broadcasted_iota(jnp.int32, (c, c), 1)
    upper = [((t_row >> (GLA_LEVELS - 1 - l)) & 1) == 1 for l in range(GLA_LEVELS)]
    pair = [((t_idx >> (GLA_LEVELS - l)) == (s_idx >> (GLA_LEVELS - l)))
            & (((t_idx >> (GLA_LEVELS - 1 - l)) & 1) == 1)
            & (((s_idx >> (GLA_LEVELS - 1 - l)) & 1) == 0) for l in range(GLA_LEVELS)]
    eye = t_idx == s_idx

    cs = range(n_chunks)
    rows = [slice(ci * c, (ci + 1) * c) for ci in cs]
    k = [(1.0 - lb) * jax.nn.sigmoid(-f_ref[rw, :]) for rw in rows]
    logf = [jnp.log1p(-jnp.minimum(x, 1.0 - 1e-6)) for x in k]
    qr = [q_ref[rw, :] for rw in rows]
    q = [x * jax.nn.sigmoid(x) for x in qr]
    d_all = [_split3_dot(amat, x) for x in logf]
    att = [jnp.where(eye, jnp.sum(a * b, axis=1, keepdims=True), 0.0) for a, b in zip(q, k)]
    for l in range(GLA_LEVELS):
        x = [(jnp.where(upper[l], a, b) * jnp.exp(d[(l + 1) * c:(l + 2) * c])).astype(BF16)
             for a, b, d in zip(q, k, d_all)]
        gram = [lax.dot_general(y, y, NT_DIMS, preferred_element_type=F32) for y in x]
        att = [a + jnp.where(pair[l], g, 0.0) for a, g in zip(att, gram)]
    gcum = [d[0:c] for d in d_all]
    g_last = [x[c - 1:c] for x in gcum]
    q_dec = [(a * jnp.exp(gc)).astype(BF16) for a, gc in zip(q, gcum)]
    k_dec = [(a * jnp.exp(gl - gc)).astype(BF16) for a, gl, gc in zip(k, g_last, gcum)]
    vb = [i_ref[rw, :].astype(BF16) for rw in rows]
    o_intra = [jnp.dot(a.astype(BF16), v, preferred_element_type=F32) for a, v in zip(att, vb)]
    kv = [lax.dot_general(v, kd, TN_DIMS, preferred_element_type=F32) for v, kd in zip(vb, k_dec)]
    e_last = [jnp.exp(x) for x in g_last]

    st = st_ref[...]
    for ci in cs:
        o = o_intra[ci] + lax.dot_general(q_dec[ci], st.astype(BF16), NT_DIMS,
                                          preferred_element_type=F32)
        st = st * e_last[ci] + kv[ci]
        o = o * lax.rsqrt(jnp.mean(o * o, axis=1, keepdims=True) + EPS) * nw
        gr = g_ref[rows[ci], :]
        o_ref[rows[ci], :] = (o * (gr * jax.nn.sigmoid(gr))).astype(o_ref.dtype)
    st_ref[...] = st

    @pl.when(r == pl.num_programs(2) - 1)
    def _():
        sfin_ref[0, 0] = st.T


def hgrn2_pallas(proj_b, row0, bsz, t, lb, norm_w, s0):
    rps = min(t, GLA_ROWS_PER_STEP)
    steps = t // rps
    blk0 = row0 // rps
    seg = lambda s: pl.BlockSpec((rps, B_DK), lambda b, h, r: (blk0 + b * steps + r, s * B_HEADS + h))
    state = pl.BlockSpec((1, 1, B_DK, B_DV), lambda b, h, r: (b, h, 0, 0))
    amat = jnp.asarray(_gla_exponent_matrix(), BF16)
    return pl.pallas_call(
        partial(_gla_body, n_chunks=rps // CHUNK),
        grid=(bsz, B_HEADS, steps),
        in_specs=[seg(0), seg(1), seg(2), seg(3),
                  pl.BlockSpec((1, B_DK), lambda b, h, r: (0, h)),
                  pl.BlockSpec((1, B_DV), lambda b, h, r: (0, 0)),
                  pl.BlockSpec(amat.shape, lambda b, h, r: (0, 0)),
                  state],
        out_specs=[pl.BlockSpec((rps, B_DV), lambda b, h, r: (b * steps + r, h)), state],
        out_shape=[jax.ShapeDtypeStruct((bsz * t, B_HEADS * B_DV), BF16),
                   jax.ShapeDtypeStruct((bsz, B_HEADS, B_DK, B_DV), F32)],
        scratch_shapes=[pltpu.VMEM((B_DV, B_DK), F32)],
        compiler_params=_params("parallel", "parallel", "arbitrary"),
        name="hgrn2",
    )(proj_b, proj_b, proj_b, proj_b, lb.reshape(1, -1).astype(F32),
      norm_w.reshape(1, -1).astype(F32), amat, s0)


GDN_PROBLEMS_PER_STEP = 32
GDN_CHUNKS_PER_STEP = 2


def _split3_dot_tn(x, u_bf16):
    hi = x.astype(BF16)
    r1 = x - hi.astype(F32)
    mid = r1.astype(BF16)
    lo = (r1 - mid.astype(F32)).astype(BF16)
    return (lax.dot_general(hi, u_bf16, TN_DIMS, preferred_element_type=F32)
            + lax.dot_general(mid, u_bf16, TN_DIMS, preferred_element_type=F32)
            + lax.dot_general(lo, u_bf16, TN_DIMS, preferred_element_type=F32))


def _gdn_body(q_ref, k_ref, v_ref, z_ref, beta_ref, g_ref, nw_ref, s0_ref, o_ref, sfin_ref, s_ref,
              *, n_chunks, heads):
    hg = pl.program_id(1)
    r = pl.program_id(2)
    c = CHUNK
    rep = C_V_HEADS // C_QK_HEADS

    @pl.when(r == 0)
    def _():
        s_ref[...] = s0_ref[0]

    nw = nw_ref[...]
    t_idx = lax.broadcasted_iota(jnp.int32, (c, c), 0)
    s_idx = lax.broadcasted_iota(jnp.int32, (c, c), 1)
    incl = s_idx <= t_idx
    strict = s_idx < t_idx
    eye = jnp.where(t_idx == s_idx, 1.0, 0.0)
    tril_b = jnp.where(incl, 1.0, 0.0).astype(BF16)
    triu_b = jnp.where(t_idx <= s_idx, 1.0, 0.0).astype(BF16)
    pair = [((t_idx >> (l + 1)) == (s_idx >> (l + 1)))
            & (((t_idx >> l) & 1) == 1) & (((s_idx >> l) & 1) == 0) for l in range(GLA_LEVELS)]
    lane_head = lax.broadcasted_iota(jnp.int32, (c, C_V_HEADS), 1)

    probs = [(j, ci) for j in range(heads) for ci in range(n_chunks)]
    cs = range(len(probs))
    rows = [slice(ci * c, (ci + 1) * c) for _, ci in probs]
    vcol = [slice(j * C_DV, (j + 1) * C_DV) for j, _ in probs]
    qcol = [slice((j // rep) * C_DK, (j // rep + 1) * C_DK) for j, _ in probs]
    head_sel = [lane_head == hg * heads + j for j, _ in probs]
    qa = [q_ref[rw, qc] for rw, qc in zip(rows, qcol)]
    ka = [k_ref[rw, qc] for rw, qc in zip(rows, qcol)]
    q = [x * lax.rsqrt(jnp.sum(x * x, axis=1, keepdims=True) + EPS) * (C_DK ** -0.5) for x in qa]
    k = [x * lax.rsqrt(jnp.sum(x * x, axis=1, keepdims=True) + EPS) for x in ka]
    beta = [jnp.sum(jnp.where(hs, beta_ref[rw, :], 0.0), axis=1, keepdims=True)
            for rw, hs in zip(rows, head_sel)]
    gb = [jnp.broadcast_to(jnp.sum(jnp.where(hs, g_ref[rw, :], 0.0), axis=1, keepdims=True),
                           (c, C_DK)) for rw, hs in zip(rows, head_sel)]
    gcum = [_split3_dot(tril_b, x) for x in gb]
    gc_row = [_split3_dot_tn(x[:, :c], triu_b) for x in gb]
    decay = [jnp.where(incl, jnp.exp(jnp.where(incl, a[:, :c] - b, 0.0)), 0.0)
             for a, b in zip(gcum, gc_row)]
    kb = [a * b for a, b in zip(k, beta)]
    kbf = [x.astype(BF16) for x in k]
    a_mat = [jnp.where(strict, lax.dot_general(a.astype(BF16), b, NT_DIMS,
                                               preferred_element_type=F32) * d, 0.0)
             for a, b, d in zip(kb, kbf, decay)]
    tinv = [eye for _ in cs]
    for l in range(GLA_LEVELS):
        tb = [x.astype(BF16) for x in tinv]
        xt = [jnp.dot(jnp.where(pair[l], a, 0.0).astype(BF16), b, preferred_element_type=F32)
              for a, b in zip(a_mat, tb)]
        tinv = [t0 - jnp.dot(b, x.astype(BF16), preferred_element_type=F32)
                for t0, b, x in zip(tinv, tb, xt)]
    e_cum = [jnp.exp(x) for x in gcum]
    rhs = [jnp.concatenate([v_ref[rw, vc] * bt, b * e], axis=1)
           for rw, vc, bt, b, e in zip(rows, vcol, beta, kb, e_cum)]
    sol = [x + jnp.dot((t0 - eye).astype(BF16), x.astype(BF16), preferred_element_type=F32)
           for x, t0 in zip(rhs, tinv)]
    qk = [jnp.where(incl, lax.dot_general(a.astype(BF16), b, NT_DIMS,
                                          preferred_element_type=F32) * d, 0.0).astype(BF16)
          for a, b, d in zip(q, kbf, decay)]
    g_last = [x[c - 1:c] for x in gcum]
    q_dec = [(a * e).astype(BF16) for a, e in zip(q, e_cum)]
    k_dec = [(a * jnp.exp(gl - gc)).astype(BF16) for a, gl, gc in zip(k, g_last, gcum)]

    s = [s_ref[j] for j in range(heads)]
    for ci in range(n_chunks):
        ps = [j * n_chunks + ci for j in range(heads)]
        sb = [x.astype(BF16) for x in s]
        ub = [(sol[p][:, :C_DV] - jnp.dot(sol[p][:, C_DV:].astype(BF16), b,
                                          preferred_element_type=F32)).astype(BF16)
              for p, b in zip(ps, sb)]
        o = [jnp.dot(q_dec[p], b, preferred_element_type=F32)
             + jnp.dot(qk[p], u, preferred_element_type=F32) for p, b, u in zip(ps, sb, ub)]
        s = [jnp.exp(g_last[p]) * x + lax.dot_general(k_dec[p], u, TN_DIMS, preferred_element_type=F32)
             for p, x, u in zip(ps, s, ub)]
        for p, x in zip(ps, o):
            x = x * lax.rsqrt(jnp.mean(x * x, axis=1, keepdims=True) + EPS) * nw
            z = z_ref[rows[p], vcol[p]]
            o_ref[rows[p], vcol[p]] = (x * (z * jax.nn.sigmoid(z))).astype(o_ref.dtype)
    for j in range(heads):
        s_ref[j] = s[j]

    @pl.when(r == pl.num_programs(2) - 1)
    def _():
        for j in range(heads):
            sfin_ref[0, j] = s[j]


def gdn_pallas(act, proj_m, beta, g, row0, bsz, t, norm_w, s0):
    n_chunks = min(t // CHUNK, GDN_CHUNKS_PER_STEP)
    heads = GDN_PROBLEMS_PER_STEP // n_chunks
    rps = n_chunks * CHUNK
    steps = t // rps
    blk0 = row0 // rps
    rep = C_V_HEADS // C_QK_HEADS
    qk_w = (heads // rep) * C_DK
    v_w = heads * C_DV
    qk_blocks = C_QK_HEADS * C_DK // qk_w
    qk_spec = lambda seg: pl.BlockSpec((rps, qk_w), lambda b, h, r: (b * steps + r, seg * qk_blocks + h))
    small = pl.BlockSpec((rps, C_V_HEADS), lambda b, h, r: (b * steps + r, 0))
    state = pl.BlockSpec((1, heads, C_DK, C_DV), lambda b, h, r: (b, h, 0, 0))
    return pl.pallas_call(
        partial(_gdn_body, n_chunks=n_chunks, heads=heads),
        grid=(bsz, C_V_HEADS // heads, steps),
        in_specs=[qk_spec(0), qk_spec(1),
                  pl.BlockSpec((rps, v_w), lambda b, h, r: (b * steps + r, 2 * C_QK_HEADS * C_DK // v_w + h)),
                  pl.BlockSpec((rps, v_w), lambda b, h, r: (blk0 + b * steps + r, C_CONV_CH // v_w + h)),
                  small, small,
                  pl.BlockSpec((1, C_DV), lambda b, h, r: (0, 0)),
                  state],
        out_specs=[pl.BlockSpec((rps, v_w), lambda b, h, r: (b * steps + r, h)), state],
        out_shape=[jax.ShapeDtypeStruct((bsz * t, C_V_HEADS * C_DV), BF16),
                   jax.ShapeDtypeStruct((bsz, C_V_HEADS, C_DK, C_DV), F32)],
        scratch_shapes=[pltpu.VMEM((heads, C_DK, C_DV), F32)],
        compiler_params=_params("parallel", "parallel", "arbitrary"),
        name="gated_deltanet",
    )(act, act, act, proj_m, beta, g, norm_w.reshape(1, -1).astype(F32), s0)


EVEN_A_COLS = 7680
OFF_AK, OFF_AV, OFF_IQ, OFF_IK, OFF_IW = 2048, 2560, 3072, 7168, 7296


def _rope_tables(pos):
    posf = pos.astype(F32)[:, None]

    def cs(half):
        inv_freq = ROPE_THETA ** (-jnp.arange(half, dtype=F32) / half)
        ang = posf * inv_freq[None, :]
        return jnp.cos(ang), jnp.sin(ang)

    c64, s64 = cs(HEAD_DIM // 2)
    c32, s32 = cs(IDX_ROPE_DIM // 2)
    z32 = jnp.zeros_like(s32)
    rest = IDX_DIM - IDX_ROPE_DIM
    ones = jnp.ones((pos.shape[0], rest), F32)
    zeros = jnp.zeros((pos.shape[0], rest), F32)
    return (jnp.concatenate([c64, c64], 1), jnp.concatenate([-s64, s64], 1),
            jnp.concatenate([c32, c32, ones], 1), jnp.concatenate([-s32, z32, zeros], 1),
            jnp.concatenate([z32, s32, zeros], 1))


def _even_prep_body(p_ref, ca_ref, sa_ref, ci_ref, s1_ref, s2_ref, qn_ref, kn_ref,
                    q_ref, k_ref, kb_ref, v_ref, vb_ref, iq_ref, ik_ref, ikb_ref, wi_ref):
    ca, sa = ca_ref[...], sa_ref[...]
    ci, s1, s2 = ci_ref[...], s1_ref[...], s2_ref[...]

    def head(off, j):
        return p_ref[:, off + j * HEAD_DIM:off + (j + 1) * HEAD_DIM]

    def norm_rope(x, w):
        y = x * lax.rsqrt(jnp.mean(x * x, axis=1, keepdims=True) + EPS) * w
        return y * ca + pltpu.roll(y, HEAD_DIM // 2, 1) * sa

    def idx_rope(x):
        half = IDX_ROPE_DIM // 2
        return x * ci + pltpu.roll(x, IDX_DIM - half, 1) * s1 + pltpu.roll(x, half, 1) * s2

    qn, kn = qn_ref[...], kn_ref[...]
    for j in range(A_HEADS):
        cols = slice(j * HEAD_DIM, (j + 1) * HEAD_DIM)
        q_ref[:, cols] = (norm_rope(head(0, j), qn) * (HEAD_DIM ** -0.5)).astype(BF16)
    for j in range(A_KV_HEADS):
        cols = slice(j * HEAD_DIM, (j + 1) * HEAD_DIM)
        kk = norm_rope(head(OFF_AK, j), kn)
        k_ref[:, cols] = kk
        kb_ref[:, cols] = kk.astype(BF16)
        vv = head(OFF_AV, j)
        v_ref[:, cols] = vv
        vb_ref[:, cols] = vv.astype(BF16)
    for j in range(IDX_HEADS):
        cols = slice(j * IDX_DIM, (j + 1) * IDX_DIM)
        iq_ref[:, cols] = idx_rope(head(OFF_IQ, j)).astype(BF16)
    ik = idx_rope(head(OFF_IK, 0))
    ik_ref[...] = ik
    ikb_ref[...] = ik.astype(BF16)
    wi_ref[...] = p_ref[:, OFF_IW:OFF_IW + IDX_HEADS] * IDX_SCALE


def even_prep(proj_a, pos, q_norm, k_norm, tm=256):
    m = proj_a.shape[0]
    tabs = _rope_tables(pos)
    row = lambda w: pl.BlockSpec((tm, w), lambda i: (i, 0))
    vec = pl.BlockSpec((1, HEAD_DIM), lambda i: (0, 0))
    kvw = A_KV_HEADS * HEAD_DIM
    shapes = [(A_HEADS * HEAD_DIM, BF16), (kvw, F32), (kvw, BF16), (kvw, F32), (kvw, BF16),
              (IDX_HEADS * IDX_DIM, BF16), (IDX_DIM, F32), (IDX_DIM, BF16), (IDX_HEADS, F32)]
    return pl.pallas_call(
        _even_prep_body,
        grid=(m // tm,),
        in_specs=[row(EVEN_A_COLS)] + [row(HEAD_DIM)] * 5 + [vec, vec],
        out_specs=[row(w) for w, _ in shapes],
        out_shape=[jax.ShapeDtypeStruct((m, w), dt) for w, dt in shapes],
        compiler_params=_params("parallel"),
        name="even_prep",
    )(proj_a, *tabs, q_norm.reshape(1, -1).astype(F32), k_norm.reshape(1, -1).astype(F32))


def _even_mixer(prep, proj_b, row0, bsz, t, past, lb, bnorm):
    m = bsz * t
    q_bf, k_f, k_bf, v_f, v_bf, iq_bf, ik_f, ik_bf, wi = [a[row0:row0 + m].reshape(bsz, t, -1) for a in prep]
    past_len = 0 if past is None else past[0].shape[1]
    if past is None:
        keys, vals, ikeys = k_bf, v_bf, ik_bf
        s0_b = jnp.zeros((bsz, B_HEADS, B_DK, B_DV), F32)
    else:
        flat = lambda a: a.reshape(bsz, past_len, -1).astype(BF16)
        keys = jnp.concatenate([flat(past[0]), k_bf], axis=1)
        vals = jnp.concatenate([flat(past[1]), v_bf], axis=1)
        ikeys = jnp.concatenate([flat(past[2]), ik_bf], axis=1)
        s0_b = past[3]
    o_a = dsa_attention_pallas(q_bf, keys, vals, iq_bf, ikeys, wi, past_len)
    o_b, s_b = hgrn2_pallas(proj_b, row0, bsz, t, lb, bnorm, s0_b)
    mixed_in = jnp.concatenate([o_a.reshape(m, -1), o_b], axis=-1)
    new_k = k_f.reshape(bsz, t, A_KV_HEADS, HEAD_DIM)
    new_v = v_f.reshape(bsz, t, A_KV_HEADS, HEAD_DIM)
    return mixed_in, (new_k, new_v, ik_f, s_b)


CONV_HALO = 8


def _conv_body(x_ref, st_ref, w_ref, o_ref, carry_ref):
    r = pl.program_id(2)
    x = x_ref[...]
    tm = x.shape[0]
    prev = jnp.where(r == 0, st_ref[0], carry_ref[...])
    top = x[0:CONV_HALO]
    row = lax.broadcasted_iota(jnp.int32, top.shape, 0)
    acc = x * w_ref[C_CONV - 1:C_CONV]
    acc_top = top * w_ref[C_CONV - 1:C_CONV]
    for k in range(1, C_CONV):
        w = w_ref[C_CONV - 1 - k:C_CONV - k]
        acc = acc + pltpu.roll(x, k, 0) * w
        acc_top = acc_top + jnp.where(row >= k, pltpu.roll(top, k, 0), pltpu.roll(prev, k, 0)) * w
    o_ref[...] = acc * jax.nn.sigmoid(acc)
    o_ref[0:CONV_HALO] = acc_top * jax.nn.sigmoid(acc_top)
    carry_ref[...] = x[tm - CONV_HALO:tm]


def conv_silu(proj_m, row0, bsz, t, conv_state, conv_w, tc=1024):
    tm = min(t, 512)
    steps = t // tm
    blk0 = row0 // tm
    st8 = jnp.pad(conv_state, ((0, 0), (CONV_HALO - (C_CONV - 1), 0), (0, 0)))
    return pl.pallas_call(
        _conv_body,
        grid=(C_CONV_CH // tc, bsz, steps),
        in_specs=[pl.BlockSpec((tm, tc), lambda c, b, r: (blk0 + b * steps + r, c)),
                  pl.BlockSpec((1, CONV_HALO, tc), lambda c, b, r: (b, 0, c)),
                  pl.BlockSpec((C_CONV, tc), lambda c, b, r: (0, c))],
        out_specs=pl.BlockSpec((tm, tc), lambda c, b, r: (b * steps + r, c)),
        out_shape=jax.ShapeDtypeStruct((bsz * t, C_CONV_CH), F32),
        scratch_shapes=[pltpu.VMEM((CONV_HALO, tc), F32)],
        compiler_params=_params("parallel", "parallel", "arbitrary"),
        name="conv_silu",
    )(proj_m, st8, conv_w.astype(F32))


def _odd_mixer(proj_m, proj_s, row0, bsz, t, past, conv_w, a_log, dt_bias, onorm):
    m = bsz * t
    ps = proj_s[row0:row0 + m]
    b_raw, a_raw = ps[:, :C_V_HEADS], ps[:, C_V_HEADS:2 * C_V_HEADS]
    if past is None:
        conv_state = jnp.zeros((bsz, C_CONV - 1, C_CONV_CH), F32)
        s0_c = jnp.zeros((bsz, C_V_HEADS, C_DK, C_DV), F32)
    else:
        s0_c, conv_state = past
    act = conv_silu(proj_m, row0, bsz, t, conv_state, conv_w)
    tail = [proj_m[row0 + b * t + t - (C_CONV - 1):row0 + (b + 1) * t, :C_CONV_CH] for b in range(bsz)]
    new_conv = jnp.stack(tail)
    beta = jax.nn.sigmoid(b_raw)
    g = -jnp.exp(a_log.astype(F32)) * jax.nn.softplus(a_raw + dt_bias.astype(F32))
    o_c, s_c = gdn_pallas(act, proj_m, beta, g, row0, bsz, t, onorm, s0_c)
    return o_c, (s_c, new_conv)


def _pad_cols(w, n):
    return jnp.pad(w, ((0, 0), (0, n - w.shape[1])))


def kernel(x_prompt, x_sample, cache_a_k, cache_a_v, cache_a_kidx, state_b, state_c, state_c_conv, p_prompt, p_sample, ffn1_norm, ffn1_w_gate, ffn1_w_up, ffn1_w_down, mix_norm, even_w_in, even_w_out, a_q_norm, a_k_norm, b_lb_logits, b_out_norm, odd_w_in, odd_w_out, c_conv_w, c_a_log, c_dt_bias, c_out_norm, ffn2_norm, ffn2_w_gate, ffn2_w_up, ffn2_w_down, ple_norm, ple_w_gate, ple_w_proj, ple_post_norm):
    d = D_MODEL
    bp, tp = x_prompt.shape[:2]
    bs, ts = x_sample.shape[:2]
    mp, ms = bp * tp, bs * ts
    x = jnp.concatenate([x_prompt.reshape(mp, d), x_sample.reshape(ms, d)], axis=0)
    p_all = jnp.concatenate([p_prompt.reshape(DEPTH, mp, -1), p_sample.reshape(DEPTH, ms, -1)],
                            axis=1).astype(BF16)

    past_len = cache_a_k.shape[2]
    pos_all = jnp.concatenate([jnp.tile(jnp.arange(tp, dtype=jnp.int32), bp),
                               jnp.tile(past_len + jnp.arange(ts, dtype=jnp.int32), bs)])

    lb_soft = jax.nn.softmax(b_lb_logits.astype(F32), axis=0)
    lbs = jnp.cumsum(lb_soft, axis=0) - lb_soft[0]

    new_p = [[] for _ in range(6)]
    new_s = [[] for _ in range(6)]
    for layer in range(DEPTH):
        j = layer // 2
        x = ffn(x, ffn1_norm[layer], ffn1_w_gate[layer].astype(BF16),
                ffn1_w_up[layer].astype(BF16), ffn1_w_down[layer].astype(BF16))
        h = rmsnorm(x, mix_norm[layer])
        if layer % 2 == 0:
            w_in = even_w_in[j]
            w_a = _pad_cols(w_in[:, :7328], 7680).astype(BF16)
            w_b = w_in[:, 7328:].astype(BF16)
            proj_a = matmul(h, w_a, name="even_in_a")
            proj_b = matmul(h, w_b, name="even_in_b")
            prep = even_prep(proj_a, pos_all, a_q_norm[j], a_k_norm[j])
            mix_p, st_p = _even_mixer(prep, proj_b, 0, bp, tp, None, lbs[j], b_out_norm[j])
            mix_s, st_s = _even_mixer(prep, proj_b, mp, bs, ts,
                                      (cache_a_k[j], cache_a_v[j], cache_a_kidx[j], state_b[j]),
                                      lbs[j], b_out_norm[j])
            for idx in range(4):
                new_p[idx].append(st_p[idx])
                new_s[idx].append(st_s[idx])
            w_out = even_w_out[j].astype(BF16)
        else:
            w_in = odd_w_in[j]
            w_m = w_in[:, :12288].astype(BF16)
            w_s = _pad_cols(w_in[:, 12288:], 128).astype(BF16)
            proj_m = matmul(h, w_m, name="odd_in_m")
            proj_s = matmul(h, w_s, tn=128, name="odd_in_s")
            mix_p, st_p = _odd_mixer(proj_m, proj_s, 0, bp, tp, None,
                                     c_conv_w[j], c_a_log[j], c_dt_bias[j], c_out_norm[j])
            mix_s, st_s = _odd_mixer(proj_m, proj_s, mp, bs, ts,
                                     (state_c[j], state_c_conv[j]),
                                     c_conv_w[j], c_a_log[j], c_dt_bias[j], c_out_norm[j])
            for idx in range(2):
                new_p[4 + idx].append(st_p[idx])
                new_s[4 + idx].append(st_s[idx])
            w_out = odd_w_out[j].astype(BF16)
        mix_in = jnp.concatenate([mix_p, mix_s], axis=0).astype(BF16)
        x = matmul(mix_in, w_out, x, body=_mm_resid_body, name="mix_out")
        x = ffn(x, ffn2_norm[layer], ffn2_w_gate[layer].astype(BF16),
                ffn2_w_up[layer].astype(BF16), ffn2_w_down[layer].astype(BF16))
        hp = rmsnorm(x, ple_norm[layer])
        pp = ple_proj(p_all[layer], ple_w_proj[layer].astype(BF16), ple_post_norm[layer])
        x = matmul(hp, ple_w_gate[layer].astype(BF16), x, pp, body=_mm_gate_body, name="ple_gate")

    y_prompt = x[:mp].reshape(bp, tp, d)
    y_sample = x[mp:].reshape(bs, ts, d)
    outs_p = tuple(jnp.stack(v) for v in new_p)
    outs_s = tuple(jnp.stack(v) for v in new_s)
    return (y_prompt, y_sample) + outs_p + outs_s
```

```python
from functools import partial

import numpy as np

import jax
import jax.numpy as jnp
from jax import lax
from jax.experimental import pallas as pl
from jax.experimental.pallas import tpu as pltpu

D_MODEL = 4096
DEPTH = 4
CHUNK = 64
QBLOCK = 128
EPS = 1e-6
NEG_BIG = -1e30
ROPE_THETA = 10000.0
HEAD_DIM = 128
A_HEADS = 16
A_KV_HEADS = 4
IDX_HEADS = 32
IDX_DIM = 128
IDX_ROPE_DIM = 64
TOPK_MAX = 256
IDX_SCALE = (IDX_HEADS * IDX_DIM) ** -0.5
B_HEADS = 16
B_DK = 128
B_DV = 128
C_QK_HEADS = 16
C_V_HEADS = 32
C_DK = 128
C_DV = 128
C_CONV = 4
C_CONV_CH = 2 * C_QK_HEADS * C_DK + C_V_HEADS * C_DV

V7X_VMEM_LIMIT_BYTES = 56 * 1024 * 1024

BF16 = jnp.bfloat16
F32 = jnp.float32


def _params(*sem):
    return pltpu.CompilerParams(dimension_semantics=sem,
                                vmem_limit_bytes=V7X_VMEM_LIMIT_BYTES)


def _rmsnorm_body(x_ref, w_ref, o_ref):
    x = x_ref[...]
    ms = jnp.mean(x * x, axis=-1, keepdims=True)
    o_ref[...] = (x * lax.rsqrt(ms + EPS) * w_ref[...]).astype(o_ref.dtype)


def rmsnorm(x, w, tm=512, out_dtype=BF16):
    m, d = x.shape
    return pl.pallas_call(
        _rmsnorm_body,
        grid=(m // tm,),
        in_specs=[pl.BlockSpec((tm, d), lambda i: (i, 0)),
                  pl.BlockSpec((1, d), lambda i: (0, 0))],
        out_specs=pl.BlockSpec((tm, d), lambda i: (i, 0)),
        out_shape=jax.ShapeDtypeStruct((m, d), out_dtype),
        compiler_params=_params("parallel"),
        name="rmsnorm",
    )(x, w.reshape(1, d).astype(F32))


def _mm_body(a_ref, w_ref, o_ref):
    o_ref[...] = jnp.dot(a_ref[...], w_ref[...], preferred_element_type=F32)


def _mm_resid_body(a_ref, w_ref, r_ref, o_ref):
    o_ref[...] = r_ref[...] + jnp.dot(a_ref[...], w_ref[...], preferred_element_type=F32)


def _mm_gate_body(a_ref, w_ref, r_ref, p_ref, o_ref):
    acc = jnp.dot(a_ref[...], w_ref[...], preferred_element_type=F32)
    o_ref[...] = r_ref[...] + jax.nn.sigmoid(acc) * p_ref[...]


def matmul(a, w, *extras, tm=1024, tn=512, body=_mm_body, name="matmul"):
    m, k = a.shape
    n = w.shape[1]
    tile = pl.BlockSpec((tm, tn), lambda i, j: (i, j))
    return pl.pallas_call(
        body,
        grid=(m // tm, n // tn),
        in_specs=[pl.BlockSpec((tm, k), lambda i, j: (i, 0)),
                  pl.BlockSpec((k, tn), lambda i, j: (0, j))] + [tile] * len(extras),
        out_specs=tile,
        out_shape=jax.ShapeDtypeStruct((m, n), F32),
        compiler_params=_params("parallel", "arbitrary"),
        name=name,
    )(a, w, *extras)


def _ple_proj_body(p_ref, w_ref, nw_ref, o_ref):
    y = jnp.dot(p_ref[...], w_ref[...], preferred_element_type=F32)
    ms = jnp.mean(y * y, axis=-1, keepdims=True)
    o_ref[...] = y * lax.rsqrt(ms + EPS) * nw_ref[...]


def ple_proj(p, w, nw, tm=512):
    m, k = p.shape
    n = w.shape[1]
    return pl.pallas_call(
        _ple_proj_body,
        grid=(m // tm,),
        in_specs=[pl.BlockSpec((tm, k), lambda i: (i, 0)),
                  pl.BlockSpec((k, n), lambda i: (0, 0)),
                  pl.BlockSpec((1, n), lambda i: (0, 0))],
        out_specs=pl.BlockSpec((tm, n), lambda i: (i, 0)),
        out_shape=jax.ShapeDtypeStruct((m, n), F32),
        compiler_params=_params("parallel"),
        name="ple_proj",
    )(p, w, nw.reshape(1, n).astype(F32))


FFN_NORM_ROWS = 64


def _ffn_body(x_ref, nw_ref, wg_ref, wu_ref, wd_ref, o_ref, h_ref):
    f = pl.program_id(1)

    @pl.when(f == 0)
    def _():
        for r0 in range(0, x_ref.shape[0], FFN_NORM_ROWS):
            rows = slice(r0, r0 + FFN_NORM_ROWS)
            x = x_ref[rows, :]
            ms = jnp.mean(x * x, axis=-1, keepdims=True)
            h_ref[rows, :] = (x * lax.rsqrt(ms + EPS) * nw_ref[...]).astype(BF16)
        o_ref[...] = jnp.zeros_like(o_ref)

    h = h_ref[...]
    g = jnp.dot(h, wg_ref[...], preferred_element_type=F32)
    u = jnp.dot(h, wu_ref[...], preferred_element_type=F32)
    act = (g * jax.nn.sigmoid(g) * u).astype(BF16)
    o_ref[...] += jnp.dot(act, wd_ref[...], preferred_element_type=F32)

    @pl.when(f == pl.num_programs(1) - 1)
    def _():
        o_ref[...] = x_ref[...] + 0.5 * o_ref[...]


def ffn(x, nw, wg, wu, wd, tm=768, tf=256):
    m, d = x.shape
    dff = wg.shape[1]
    return pl.pallas_call(
        _ffn_body,
        grid=(m // tm, dff // tf),
        in_specs=[pl.BlockSpec((tm, d), lambda i, f: (i, 0), pipeline_mode=pl.Buffered(1)),
                  pl.BlockSpec((1, d), lambda i, f: (0, 0)),
                  pl.BlockSpec((d, tf), lambda i, f: (0, f)),
                  pl.BlockSpec((d, tf), lambda i, f: (0, f)),
                  pl.BlockSpec((tf, d), lambda i, f: (f, 0))],
        out_specs=pl.BlockSpec((tm, d), lambda i, f: (i, 0)),
        out_shape=jax.ShapeDtypeStruct((m, d), F32),
        scratch_shapes=[pltpu.VMEM((tm, d), BF16)],
        compiler_params=_params("parallel", "arbitrary"),
        name="ffn",
    )(x, nw.reshape(1, d).astype(F32), wg, wu, wd)


DSA_KEY_TILE = 512
DSA_HEAD_GROUP = 8
DSA_HEADS_INTERLEAVED = 4
INT32_MIN = -2 ** 31
A_GROUP = A_HEADS // A_KV_HEADS
NT_DIMS = (((1,), (1,)), ((), ()))


def _dsa_body(qi_ref, wi_ref, qt_ref, ki_ref, k_ref, vt_ref, o_ref,
              qi_st, wib, skey, m_s, l_s, acc_s, *, qb, past_len, l_valid, topk, idx_bits):
    lk = DSA_KEY_TILE
    i = pl.program_id(1)
    q_pos0 = past_len + i * qb
    limit = jnp.minimum(((q_pos0 + qb - 1) // CHUNK + 1) * CHUNK, l_valid)
    nt = (limit + lk - 1) // lk

    for h in range(IDX_HEADS):
        qi_st[h * qb:(h + 1) * qb, :] = qi_ref[0, :, h * IDX_DIM:(h + 1) * IDX_DIM]
        wib[h] = jnp.broadcast_to(wi_ref[0, :, h:h + 1], (qb, 128))

    q_chunk = (q_pos0 + lax.broadcasted_iota(jnp.int32, (qb, 128), 0)) // CHUNK

    def score_tile(t, carry):
        kt = ki_ref[0, t]
        cols = [jnp.zeros((qb, 128), F32) for _ in range(lk // 128)]
        for hg in range(IDX_HEADS // DSA_HEAD_GROUP):
            rows = DSA_HEAD_GROUP * qb
            lg = lax.dot_general(qi_st[hg * rows:(hg + 1) * rows, :], kt, NT_DIMS,
                                 preferred_element_type=F32)
            for hh in range(DSA_HEAD_GROUP):
                w = wib[hg * DSA_HEAD_GROUP + hh]
                for c in range(lk // 128):
                    blk = lg[hh * qb:(hh + 1) * qb, c * 128:(c + 1) * 128]
                    cols[c] = cols[c] + jnp.maximum(blk, 0.0) * w
        for c in range(lk // 128):
            k_pos = t * lk + c * 128 + lax.broadcasted_iota(jnp.int32, (qb, 128), 1)
            adm = (k_pos // CHUNK <= q_chunk) & (k_pos < l_valid)
            bits = lax.bitcast_convert_type(cols[c], jnp.int32)
            key = jnp.where(bits < 0, bits ^ jnp.int32(0x7FFFFFFF), bits)
            skey[t, :, c * 128:(c + 1) * 128] = jnp.where(adm, key, jnp.int32(INT32_MIN))
        return carry

    lax.fori_loop(0, nt, score_tile, 0)

    def count_ge(cand):
        def body(t, acc):
            for c in range(lk // 128):
                acc = acc + jnp.where(skey[t, :, c * 128:(c + 1) * 128] >= cand, 1.0, 0.0)
            return acc
        acc = lax.fori_loop(0, nt, body, jnp.zeros((qb, 128), F32))
        return jnp.sum(acc, axis=1, keepdims=True)

    kf = jnp.float32(topk)
    zero = jnp.zeros((qb, 128), jnp.int32)
    prefix = jnp.where(count_ge(zero) >= kf, zero, jnp.int32(INT32_MIN))

    def bit_step(s, prefix):
        cand = prefix + lax.shift_left(jnp.int32(1), jnp.int32(30) - s)
        return jnp.where(count_ge(cand) >= kf, cand, prefix)

    prefix = lax.fori_loop(0, 31, bit_step, prefix)
    thr = jnp.maximum(prefix, jnp.int32(INT32_MIN + 1))

    real = prefix > jnp.int32(INT32_MIN)
    tie_rows = jnp.where((count_ge(prefix) > kf) & real[:, 0:1], 1.0, 0.0)

    @pl.when(jnp.max(tie_rows) > 0.0)
    def _():
        lane = lax.broadcasted_iota(jnp.int32, (qb, 128), 1)

        def count_tiles(pred):
            def body(t, acc):
                for c in range(lk // 128):
                    tile = skey[t, :, c * 128:(c + 1) * 128]
                    acc = acc + jnp.where(pred(tile, t * lk + c * 128 + lane), 1.0, 0.0)
                return acc
            acc = lax.fori_loop(0, nt, body, jnp.zeros((qb, 128), F32))
            return jnp.sum(acc, axis=1, keepdims=True)

        need = kf - count_tiles(lambda tile, kidx: tile > prefix)

        def cut_step(s, cut):
            cand = cut + lax.shift_left(jnp.int32(1), jnp.int32(idx_bits - 1) - s)
            before = count_tiles(lambda tile, kidx: (tile == prefix) & (kidx < cand))
            return jnp.where(before < need, cand, cut)

        cut = lax.fori_loop(0, idx_bits, cut_step, jnp.zeros((qb, 128), jnp.int32))

        def demote(t, carry):
            for c in range(lk // 128):
                cols = slice(c * 128, (c + 1) * 128)
                tile = skey[t, :, cols]
                drop = real & (tile == prefix) & (t * lk + c * 128 + lane > cut)
                skey[t, :, cols] = jnp.where(drop, prefix - 1, tile)
            return carry

        lax.fori_loop(0, nt, demote, 0)

    m_s[...] = jnp.full(m_s.shape, NEG_BIG, F32)
    l_s[...] = jnp.zeros(l_s.shape, F32)
    acc_s[...] = jnp.zeros(acc_s.shape, F32)

    eye_q = jnp.where(lax.broadcasted_iota(jnp.int32, (qb, qb), 0)
                      == lax.broadcasted_iota(jnp.int32, (qb, qb), 1), 1.0, 0.0).astype(BF16)
    eye_d = jnp.where(lax.broadcasted_iota(jnp.int32, (HEAD_DIM, HEAD_DIM), 0)
                      == lax.broadcasted_iota(jnp.int32, (HEAD_DIM, HEAD_DIM), 1), 1.0, 0.0).astype(BF16)

    def attn_tile(t, carry):
        sel = jnp.concatenate(
            [jnp.where(skey[t, :, c * 128:(c + 1) * 128] >= thr, 1.0, 0.0)
             for c in range(lk // 128)], axis=1).astype(BF16)
        sel_t = lax.dot_general(sel, eye_q, TN_DIMS, preferred_element_type=F32)
        bias = (sel_t - 1.0) * (-NEG_BIG)
        bias4 = jnp.concatenate([bias] * A_GROUP, axis=1)
        for n0 in range(0, A_KV_HEADS, DSA_HEADS_INTERLEAVED):
            ns = range(n0, n0 + DSA_HEADS_INTERLEAVED)
            s = [jnp.dot(k_ref[0, t, :, n * HEAD_DIM:(n + 1) * HEAD_DIM], qt_ref[0, 0, n],
                         preferred_element_type=F32) + bias4 for n in ns]
            m_old = [m_s[n] for n in ns]
            m_new = [jnp.maximum(mo, jnp.max(x, axis=0, keepdims=True)) for mo, x in zip(m_old, s)]
            alpha = [jnp.exp(mo - mn) for mo, mn in zip(m_old, m_new)]
            p = [jnp.exp(x - mn) for x, mn in zip(s, m_new)]
            pv = [jnp.dot(vt_ref[0, t, n], x.astype(BF16), preferred_element_type=F32)
                  for n, x in zip(ns, p)]
            for j, n in enumerate(ns):
                l_s[n] = alpha[j] * l_s[n] + jnp.sum(p[j], axis=0, keepdims=True)
                acc_s[n] = alpha[j] * acc_s[n] + pv[j]
                m_s[n] = m_new[j]
        return carry

    lax.fori_loop(0, nt, attn_tile, 0)

    for n in range(A_KV_HEADS):
        o = (acc_s[n] / l_s[n]).astype(BF16)
        for g in range(A_GROUP):
            hd = n * A_GROUP + g
            o_t = lax.dot_general(o[:, g * qb:(g + 1) * qb], eye_d, TN_DIMS,
                                  preferred_element_type=F32)
            o_ref[0, :, hd * HEAD_DIM:(hd + 1) * HEAD_DIM] = o_t.astype(o_ref.dtype)


def dsa_attention_pallas(q, k, v, qi, ki, wi, past_len):
    bsz, t, _ = q.shape
    l = k.shape[1]
    lk = DSA_KEY_TILE
    topk = min(TOPK_MAX, l // 4)
    qb = min(QBLOCK, t)
    n_tiles = -(-l // lk)
    pad = n_tiles * lk - l

    def tiles(a):
        a = jnp.pad(a, ((0, 0), (0, pad), (0, 0))).astype(BF16)
        return a.reshape(bsz, n_tiles, lk, a.shape[-1])

    nb = t // qb
    qs = q.reshape(bsz, nb, qb, A_KV_HEADS, A_GROUP, HEAD_DIM)
    qt = jnp.transpose(qs, (0, 1, 3, 5, 4, 2)).reshape(bsz, nb, A_KV_HEADS, HEAD_DIM, A_GROUP * qb)
    vt = jnp.transpose(tiles(v).reshape(bsz, n_tiles, lk, A_KV_HEADS, HEAD_DIM), (0, 1, 3, 4, 2))
    whole = lambda shape: pl.BlockSpec(shape, lambda b, i: (b,) + (0,) * (len(shape) - 1),
                                       pipeline_mode=pl.Buffered(1))
    rows = lambda width: pl.BlockSpec((1, qb, width), lambda b, i: (b, i, 0))
    body = partial(_dsa_body, qb=qb, past_len=past_len, l_valid=l, topk=topk,
                   idx_bits=(n_tiles * lk - 1).bit_length())
    return pl.pallas_call(
        body,
        grid=(bsz, nb),
        in_specs=[rows(IDX_HEADS * IDX_DIM), rows(IDX_HEADS),
                  pl.BlockSpec((1, 1, A_KV_HEADS, HEAD_DIM, A_GROUP * qb), lambda b, i: (b, i, 0, 0, 0)),
                  whole((1, n_tiles, lk, IDX_DIM)),
                  whole((1, n_tiles, lk, A_KV_HEADS * HEAD_DIM)),
                  whole((1, n_tiles, A_KV_HEADS, HEAD_DIM, lk))],
        out_specs=rows(A_HEADS * HEAD_DIM),
        out_shape=jax.ShapeDtypeStruct((bsz, t, A_HEADS * HEAD_DIM), BF16),
        scratch_shapes=[pltpu.VMEM((IDX_HEADS * qb, IDX_DIM), BF16),
                        pltpu.VMEM((IDX_HEADS, qb, 128), F32),
                        pltpu.VMEM((n_tiles, qb, lk), jnp.int32),
                        pltpu.VMEM((A_KV_HEADS, 1, A_GROUP * qb), F32),
                        pltpu.VMEM((A_KV_HEADS, 1, A_GROUP * qb), F32),
                        pltpu.VMEM((A_KV_HEADS, HEAD_DIM, A_GROUP * qb), F32)],
        compiler_params=_params("parallel", "arbitrary"),
        name="dsa_attention",
    )(qi, wi, qt, tiles(ki), tiles(k), vt)


GLA_LEVELS = 6
GLA_ROWS_PER_STEP = 512
GLA_PROBLEMS_PER_STEP = 16


def _gla_exponent_matrix():
    c = CHUNK
    a = np.zeros((GLA_LEVELS + 1, c, c), np.float32)
    a[0] = np.tril(np.ones((c, c), np.float32))
    for l in range(GLA_LEVELS):
        b = (c // 2) >> l
        for t in range(c):
            ref = (t // (2 * b)) * 2 * b + b
            if t >= ref:
                a[l + 1, t, ref + 1:t + 1] = 1.0
            else:
                a[l + 1, t, t + 1:ref + 1] = 1.0
    return a.reshape((GLA_LEVELS + 1) * c, c)


def _split3_dot(a_bf16, x):
    hi = x.astype(BF16)
    r1 = x - hi.astype(F32)
    mid = r1.astype(BF16)
    lo = (r1 - mid.astype(F32)).astype(BF16)
    return (jnp.dot(a_bf16, hi, preferred_element_type=F32)
            + jnp.dot(a_bf16, mid, preferred_element_type=F32)
            + jnp.dot(a_bf16, lo, preferred_element_type=F32))


TN_DIMS = (((0,), (0,)), ((), ()))


def _gla_body(q_ref, f_ref, i_ref, g_ref, lb_ref, nw_ref, a_ref, s0_ref, o_ref, sfin_ref, st_ref,
              *, n_chunks, heads):
    r = pl.program_id(2)
    c = CHUNK

    @pl.when(r == 0)
    def _():
        for j in range(heads):
            st_ref[j] = s0_ref[0, j].T

    nw = nw_ref[...]
    amat = a_ref[...]
    t_row = lax.broadcasted_iota(jnp.int32, (c, B_DK), 0)
    t_idx = lax.broadcasted_iota(jnp.int32, (c, c), 0)
    s_idx = lax.broadcasted_iota(jnp.int32, (c, c), 1)
    upper = [((t_row >> (GLA_LEVELS - 1 - l)) & 1) == 1 for l in range(GLA_LEVELS)]
    pair = [((t_idx >> (GLA_LEVELS - l)) == (s_idx >> (GLA_LEVELS - l)))
            & (((t_idx >> (GLA_LEVELS - 1 - l)) & 1) == 1)
            & (((s_idx >> (GLA_LEVELS - 1 - l)) & 1) == 0) for l in range(GLA_LEVELS)]
    eye = t_idx == s_idx

    probs = [(j, ci) for j in range(heads) for ci in range(n_chunks)]
    rows = [slice(ci * c, (ci + 1) * c) for _, ci in probs]
    cols = [slice(j * B_DK, (j + 1) * B_DK) for j, _ in probs]
    k = [(1.0 - lb_ref[:, cl]) * jax.nn.sigmoid(-f_ref[rw, cl]) for rw, cl in zip(rows, cols)]
    logf = [jnp.log1p(-jnp.minimum(x, 1.0 - 1e-6)) for x in k]
    qr = [q_ref[rw, cl] for rw, cl in zip(rows, cols)]
    q = [x * jax.nn.sigmoid(x) for x in qr]
    d_all = [_split3_dot(amat, x) for x in logf]
    att = [jnp.where(eye, jnp.sum(a * b, axis=1, keepdims=True), 0.0) for a, b in zip(q, k)]
    for l in range(GLA_LEVELS):
        x = [(jnp.where(upper[l], a, b) * jnp.exp(d[(l + 1) * c:(l + 2) * c])).astype(BF16)
             for a, b, d in zip(q, k, d_all)]
        gram = [lax.dot_general(y, y, NT_DIMS, preferred_element_type=F32) for y in x]
        att = [a + jnp.where(pair[l], g, 0.0) for a, g in zip(att, gram)]
    gcum = [d[0:c] for d in d_all]
    g_last = [x[c - 1:c] for x in gcum]
    q_dec = [(a * jnp.exp(gc)).astype(BF16) for a, gc in zip(q, gcum)]
    k_dec = [(a * jnp.exp(gl - gc)).astype(BF16) for a, gl, gc in zip(k, g_last, gcum)]
    vb = [i_ref[rw, cl].astype(BF16) for rw, cl in zip(rows, cols)]
    o_intra = [jnp.dot(a.astype(BF16), v, preferred_element_type=F32) for a, v in zip(att, vb)]
    kv = [lax.dot_general(v, kd, TN_DIMS, preferred_element_type=F32) for v, kd in zip(vb, k_dec)]
    e_last = [jnp.exp(x) for x in g_last]

    st = [st_ref[j] for j in range(heads)]
    for ci in range(n_chunks):
        ps = [j * n_chunks + ci for j in range(heads)]
        o = [o_intra[p] + lax.dot_general(q_dec[p], x.astype(BF16), NT_DIMS, preferred_element_type=F32)
             for p, x in zip(ps, st)]
        st = [x * e_last[p] + kv[p] for p, x in zip(ps, st)]
        for p, x in zip(ps, o):
            x = x * lax.rsqrt(jnp.mean(x * x, axis=1, keepdims=True) + EPS) * nw
            gr = g_ref[rows[p], cols[p]]
            o_ref[rows[p], cols[p]] = (x * (gr * jax.nn.sigmoid(gr))).astype(o_ref.dtype)
    for j in range(heads):
        st_ref[j] = st[j]

    @pl.when(r == pl.num_programs(2) - 1)
    def _():
        for j in range(heads):
            sfin_ref[0, j] = st[j].T


def hgrn2_pallas(proj_b, row0, bsz, t, lb, norm_w, s0):
    rps = min(t, GLA_ROWS_PER_STEP)
    n_chunks = rps // CHUNK
    heads = min(B_HEADS, GLA_PROBLEMS_PER_STEP // n_chunks)
    steps = t // rps
    blk0 = row0 // rps
    groups = B_HEADS // heads
    width = heads * B_DK
    seg = lambda s: pl.BlockSpec((rps, width), lambda b, h, r: (blk0 + b * steps + r, s * groups + h))
    state = pl.BlockSpec((1, heads, B_DK, B_DV), lambda b, h, r: (b, h, 0, 0))
    amat = jnp.asarray(_gla_exponent_matrix(), BF16)
    return pl.pallas_call(
        partial(_gla_body, n_chunks=n_chunks, heads=heads),
        grid=(bsz, groups, steps),
        in_specs=[seg(0), seg(1), seg(2), seg(3),
                  pl.BlockSpec((1, width), lambda b, h, r: (0, h)),
                  pl.BlockSpec((1, B_DV), lambda b, h, r: (0, 0)),
                  pl.BlockSpec(amat.shape, lambda b, h, r: (0, 0)),
                  state],
        out_specs=[pl.BlockSpec((rps, width), lambda b, h, r: (b * steps + r, h)), state],
        out_shape=[jax.ShapeDtypeStruct((bsz * t, B_HEADS * B_DV), BF16),
                   jax.ShapeDtypeStruct((bsz, B_HEADS, B_DK, B_DV), F32)],
        scratch_shapes=[pltpu.VMEM((heads, B_DV, B_DK), F32)],
        compiler_params=_params("parallel", "parallel", "arbitrary"),
        name="hgrn2",
    )(proj_b, proj_b, proj_b, proj_b, lb.reshape(1, -1).astype(F32),
      norm_w.reshape(1, -1).astype(F32), amat, s0)


GDN_PROBLEMS_PER_STEP = 32
GDN_CHUNKS_PER_STEP = 2


def _split3_dot_tn(x, u_bf16):
    hi = x.astype(BF16)
    r1 = x - hi.astype(F32)
    mid = r1.astype(BF16)
    lo = (r1 - mid.astype(F32)).astype(BF16)
    return (lax.dot_general(hi, u_bf16, TN_DIMS, preferred_element_type=F32)
            + lax.dot_general(mid, u_bf16, TN_DIMS, preferred_element_type=F32)
            + lax.dot_general(lo, u_bf16, TN_DIMS, preferred_element_type=F32))


def _gdn_body(q_ref, k_ref, v_ref, z_ref, beta_ref, g_ref, nw_ref, s0_ref, o_ref, sfin_ref, s_ref,
              *, n_chunks, heads):
    hg = pl.program_id(1)
    r = pl.program_id(2)
    c = CHUNK
    rep = C_V_HEADS // C_QK_HEADS

    @pl.when(r == 0)
    def _():
        s_ref[...] = s0_ref[0]

    nw = nw_ref[...]
    t_idx = lax.broadcasted_iota(jnp.int32, (c, c), 0)
    s_idx = lax.broadcasted_iota(jnp.int32, (c, c), 1)
    incl = s_idx <= t_idx
    strict = s_idx < t_idx
    eye = jnp.where(t_idx == s_idx, 1.0, 0.0)
    tril_b = jnp.where(incl, 1.0, 0.0).astype(BF16)
    triu_b = jnp.where(t_idx <= s_idx, 1.0, 0.0).astype(BF16)
    pair = [((t_idx >> (l + 1)) == (s_idx >> (l + 1)))
            & (((t_idx >> l) & 1) == 1) & (((s_idx >> l) & 1) == 0) for l in range(GLA_LEVELS)]
    lane_head = lax.broadcasted_iota(jnp.int32, (c, C_V_HEADS), 1)

    probs = [(j, ci) for j in range(heads) for ci in range(n_chunks)]
    cs = range(len(probs))
    rows = [slice(ci * c, (ci + 1) * c) for _, ci in probs]
    vcol = [slice(j * C_DV, (j + 1) * C_DV) for j, _ in probs]
    qcol = [slice((j // rep) * C_DK, (j // rep + 1) * C_DK) for j, _ in probs]
    head_sel = [lane_head == hg * heads + j for j, _ in probs]
    qa = [q_ref[rw, qc] for rw, qc in zip(rows, qcol)]
    ka = [k_ref[rw, qc] for rw, qc in zip(rows, qcol)]
    q = [x * lax.rsqrt(jnp.sum(x * x, axis=1, keepdims=True) + EPS) * (C_DK ** -0.5) for x in qa]
    k = [x * lax.rsqrt(jnp.sum(x * x, axis=1, keepdims=True) + EPS) for x in ka]
    beta = [jnp.sum(jnp.where(hs, beta_ref[rw, :], 0.0), axis=1, keepdims=True)
            for rw, hs in zip(rows, head_sel)]
    gb = [jnp.broadcast_to(jnp.sum(jnp.where(hs, g_ref[rw, :], 0.0), axis=1, keepdims=True),
                           (c, C_DK)) for rw, hs in zip(rows, head_sel)]
    gcum = [_split3_dot(tril_b, x) for x in gb]
    gc_row = [_split3_dot_tn(x[:, :c], triu_b) for x in gb]
    decay = [jnp.where(incl, jnp.exp(jnp.where(incl, a[:, :c] - b, 0.0)), 0.0)
             for a, b in zip(gcum, gc_row)]
    kb = [a * b for a, b in zip(k, beta)]
    kbf = [x.astype(BF16) for x in k]
    a_mat = [jnp.where(strict, lax.dot_general(a.astype(BF16), b, NT_DIMS,
                                               preferred_element_type=F32) * d, 0.0)
             for a, b, d in zip(kb, kbf, decay)]
    tinv = [eye for _ in cs]
    for l in range(GLA_LEVELS):
        tb = [x.astype(BF16) for x in tinv]
        xt = [jnp.dot(jnp.where(pair[l], a, 0.0).astype(BF16), b, preferred_element_type=F32)
              for a, b in zip(a_mat, tb)]
        tinv = [t0 - jnp.dot(b, x.astype(BF16), preferred_element_type=F32)
                for t0, b, x in zip(tinv, tb, xt)]
    e_cum = [jnp.exp(x) for x in gcum]
    rhs = [jnp.concatenate([v_ref[rw, vc] * bt, b * e], axis=1)
           for rw, vc, bt, b, e in zip(rows, vcol, beta, kb, e_cum)]
    sol = [x + jnp.dot((t0 - eye).astype(BF16), x.astype(BF16), preferred_element_type=F32)
           for x, t0 in zip(rhs, tinv)]
    qk = [jnp.where(incl, lax.dot_general(a.astype(BF16), b, NT_DIMS,
                                          preferred_element_type=F32) * d, 0.0).astype(BF16)
          for a, b, d in zip(q, kbf, decay)]
    g_last = [x[c - 1:c] for x in gcum]
    q_dec = [(a * e).astype(BF16) for a, e in zip(q, e_cum)]
    k_dec = [(a * jnp.exp(gl - gc)).astype(BF16) for a, gl, gc in zip(k, g_last, gcum)]

    s = [s_ref[j] for j in range(heads)]
    for ci in range(n_chunks):
        ps = [j * n_chunks + ci for j in range(heads)]
        sb = [x.astype(BF16) for x in s]
        ub = [(sol[p][:, :C_DV] - jnp.dot(sol[p][:, C_DV:].astype(BF16), b,
                                          preferred_element_type=F32)).astype(BF16)
              for p, b in zip(ps, sb)]
        o = [jnp.dot(q_dec[p], b, preferred_element_type=F32)
             + jnp.dot(qk[p], u, preferred_element_type=F32) for p, b, u in zip(ps, sb, ub)]
        s = [jnp.exp(g_last[p]) * x + lax.dot_general(k_dec[p], u, TN_DIMS, preferred_element_type=F32)
             for p, x, u in zip(ps, s, ub)]
        for p, x in zip(ps, o):
            x = x * lax.rsqrt(jnp.mean(x * x, axis=1, keepdims=True) + EPS) * nw
            z = z_ref[rows[p], vcol[p]]
            o_ref[rows[p], vcol[p]] = (x * (z * jax.nn.sigmoid(z))).astype(o_ref.dtype)
    for j in range(heads):
        s_ref[j] = s[j]

    @pl.when(r == pl.num_programs(2) - 1)
    def _():
        for j in range(heads):
            sfin_ref[0, j] = s[j]


def gdn_pallas(act, proj_m, beta, g, row0, bsz, t, norm_w, s0):
    n_chunks = min(t // CHUNK, GDN_CHUNKS_PER_STEP)
    heads = GDN_PROBLEMS_PER_STEP // n_chunks
    rps = n_chunks * CHUNK
    steps = t // rps
    blk0 = row0 // rps
    rep = C_V_HEADS // C_QK_HEADS
    qk_w = (heads // rep) * C_DK
    v_w = heads * C_DV
    qk_blocks = C_QK_HEADS * C_DK // qk_w
    qk_spec = lambda seg: pl.BlockSpec((rps, qk_w), lambda b, h, r: (b * steps + r, seg * qk_blocks + h))
    small = pl.BlockSpec((rps, C_V_HEADS), lambda b, h, r: (b * steps + r, 0))
    state = pl.BlockSpec((1, heads, C_DK, C_DV), lambda b, h, r: (b, h, 0, 0))
    return pl.pallas_call(
        partial(_gdn_body, n_chunks=n_chunks, heads=heads),
        grid=(bsz, C_V_HEADS // heads, steps),
        in_specs=[qk_spec(0), qk_spec(1),
                  pl.BlockSpec((rps, v_w), lambda b, h, r: (b * steps + r, 2 * C_QK_HEADS * C_DK // v_w + h)),
                  pl.BlockSpec((rps, v_w), lambda b, h, r: (blk0 + b * steps + r, C_CONV_CH // v_w + h)),
                  small, small,
                  pl.BlockSpec((1, C_DV), lambda b, h, r: (0, 0)),
                  state],
        out_specs=[pl.BlockSpec((rps, v_w), lambda b, h, r: (b * steps + r, h)), state],
        out_shape=[jax.ShapeDtypeStruct((bsz * t, C_V_HEADS * C_DV), BF16),
                   jax.ShapeDtypeStruct((bsz, C_V_HEADS, C_DK, C_DV), F32)],
        scratch_shapes=[pltpu.VMEM((heads, C_DK, C_DV), F32)],
        compiler_params=_params("parallel", "parallel", "arbitrary"),
        name="gated_deltanet",
    )(act, act, act, proj_m, beta, g, norm_w.reshape(1, -1).astype(F32), s0)


EVEN_A_COLS = 7680
OFF_AK, OFF_AV, OFF_IQ, OFF_IK, OFF_IW = 2048, 2560, 3072, 7168, 7296


def _rope_tables(pos):
    posf = pos.astype(F32)[:, None]

    def cs(half):
        inv_freq = ROPE_THETA ** (-jnp.arange(half, dtype=F32) / half)
        ang = posf * inv_freq[None, :]
        return jnp.cos(ang), jnp.sin(ang)

    c64, s64 = cs(HEAD_DIM // 2)
    c32, s32 = cs(IDX_ROPE_DIM // 2)
    z32 = jnp.zeros_like(s32)
    rest = IDX_DIM - IDX_ROPE_DIM
    ones = jnp.ones((pos.shape[0], rest), F32)
    zeros = jnp.zeros((pos.shape[0], rest), F32)
    return (jnp.concatenate([c64, c64], 1), jnp.concatenate([-s64, s64], 1),
            jnp.concatenate([c32, c32, ones], 1), jnp.concatenate([-s32, z32, zeros], 1),
            jnp.concatenate([z32, s32, zeros], 1))


def _even_prep_body(p_ref, ca_ref, sa_ref, ci_ref, s1_ref, s2_ref, qn_ref, kn_ref,
                    q_ref, k_ref, kb_ref, v_ref, vb_ref, iq_ref, ik_ref, ikb_ref, wi_ref):
    ca, sa = ca_ref[...], sa_ref[...]
    ci, s1, s2 = ci_ref[...], s1_ref[...], s2_ref[...]

    def head(off, j):
        return p_ref[:, off + j * HEAD_DIM:off + (j + 1) * HEAD_DIM]

    def norm_rope(x, w):
        y = x * lax.rsqrt(jnp.mean(x * x, axis=1, keepdims=True) + EPS) * w
        return y * ca + pltpu.roll(y, HEAD_DIM // 2, 1) * sa

    def idx_rope(x):
        half = IDX_ROPE_DIM // 2
        return x * ci + pltpu.roll(x, IDX_DIM - half, 1) * s1 + pltpu.roll(x, half, 1) * s2

    qn, kn = qn_ref[...], kn_ref[...]
    for j in range(A_HEADS):
        cols = slice(j * HEAD_DIM, (j + 1) * HEAD_DIM)
        q_ref[:, cols] = (norm_rope(head(0, j), qn) * (HEAD_DIM ** -0.5)).astype(BF16)
    for j in range(A_KV_HEADS):
        cols = slice(j * HEAD_DIM, (j + 1) * HEAD_DIM)
        kk = norm_rope(head(OFF_AK, j), kn)
        k_ref[:, cols] = kk
        kb_ref[:, cols] = kk.astype(BF16)
        vv = head(OFF_AV, j)
        v_ref[:, cols] = vv
        vb_ref[:, cols] = vv.astype(BF16)
    for j in range(IDX_HEADS):
        cols = slice(j * IDX_DIM, (j + 1) * IDX_DIM)
        iq_ref[:, cols] = idx_rope(head(OFF_IQ, j)).astype(BF16)
    ik = idx_rope(head(OFF_IK, 0))
    ik_ref[...] = ik
    ikb_ref[...] = ik.astype(BF16)
    wi_ref[...] = p_ref[:, OFF_IW:OFF_IW + IDX_HEADS] * IDX_SCALE


def even_prep(proj_a, pos, q_norm, k_norm, tm=256):
    m = proj_a.shape[0]
    tabs = _rope_tables(pos)
    row = lambda w: pl.BlockSpec((tm, w), lambda i: (i, 0))
    vec = pl.BlockSpec((1, HEAD_DIM), lambda i: (0, 0))
    kvw = A_KV_HEADS * HEAD_DIM
    shapes = [(A_HEADS * HEAD_DIM, BF16), (kvw, F32), (kvw, BF16), (kvw, F32), (kvw, BF16),
              (IDX_HEADS * IDX_DIM, BF16), (IDX_DIM, F32), (IDX_DIM, BF16), (IDX_HEADS, F32)]
    return pl.pallas_call(
        _even_prep_body,
        grid=(m // tm,),
        in_specs=[row(EVEN_A_COLS)] + [row(HEAD_DIM)] * 5 + [vec, vec],
        out_specs=[row(w) for w, _ in shapes],
        out_shape=[jax.ShapeDtypeStruct((m, w), dt) for w, dt in shapes],
        compiler_params=_params("parallel"),
        name="even_prep",
    )(proj_a, *tabs, q_norm.reshape(1, -1).astype(F32), k_norm.reshape(1, -1).astype(F32))


def _even_mixer(prep, proj_b, row0, bsz, t, past, lb, bnorm):
    m = bsz * t
    q_bf, k_f, k_bf, v_f, v_bf, iq_bf, ik_f, ik_bf, wi = [a[row0:row0 + m].reshape(bsz, t, -1) for a in prep]
    past_len = 0 if past is None else past[0].shape[1]
    if past is None:
        keys, vals, ikeys = k_bf, v_bf, ik_bf
        s0_b = jnp.zeros((bsz, B_HEADS, B_DK, B_DV), F32)
    else:
        flat = lambda a: a.reshape(bsz, past_len, -1).astype(BF16)
        keys = jnp.concatenate([flat(past[0]), k_bf], axis=1)
        vals = jnp.concatenate([flat(past[1]), v_bf], axis=1)
        ikeys = jnp.concatenate([flat(past[2]), ik_bf], axis=1)
        s0_b = past[3]
    o_a = dsa_attention_pallas(q_bf, keys, vals, iq_bf, ikeys, wi, past_len)
    o_b, s_b = hgrn2_pallas(proj_b, row0, bsz, t, lb, bnorm, s0_b)
    mixed_in = jnp.concatenate([o_a.reshape(m, -1), o_b], axis=-1)
    new_k = k_f.reshape(bsz, t, A_KV_HEADS, HEAD_DIM)
    new_v = v_f.reshape(bsz, t, A_KV_HEADS, HEAD_DIM)
    return mixed_in, (new_k, new_v, ik_f, s_b)


CONV_HALO = 8


def _conv_body(x_ref, st_ref, w_ref, o_ref, carry_ref):
    r = pl.program_id(2)
    x = x_ref[...]
    tm = x.shape[0]
    prev = jnp.where(r == 0, st_ref[0], carry_ref[...])
    top = x[0:CONV_HALO]
    row = lax.broadcasted_iota(jnp.int32, top.shape, 0)
    acc = x * w_ref[C_CONV - 1:C_CONV]
    acc_top = top * w_ref[C_CONV - 1:C_CONV]
    for k in range(1, C_CONV):
        w = w_ref[C_CONV - 1 - k:C_CONV - k]
        acc = acc + pltpu.roll(x, k, 0) * w
        acc_top = acc_top + jnp.where(row >= k, pltpu.roll(top, k, 0), pltpu.roll(prev, k, 0)) * w
    o_ref[...] = acc * jax.nn.sigmoid(acc)
    o_ref[0:CONV_HALO] = acc_top * jax.nn.sigmoid(acc_top)
    carry_ref[...] = x[tm - CONV_HALO:tm]


def conv_silu(proj_m, row0, bsz, t, conv_state, conv_w, tc=1024):
    tm = min(t, 512)
    steps = t // tm
    blk0 = row0 // tm
    st8 = jnp.pad(conv_state, ((0, 0), (CONV_HALO - (C_CONV - 1), 0), (0, 0)))
    return pl.pallas_call(
        _conv_body,
        grid=(C_CONV_CH // tc, bsz, steps),
        in_specs=[pl.BlockSpec((tm, tc), lambda c, b, r: (blk0 + b * steps + r, c)),
                  pl.BlockSpec((1, CONV_HALO, tc), lambda c, b, r: (b, 0, c)),
                  pl.BlockSpec((C_CONV, tc), lambda c, b, r: (0, c))],
        out_specs=pl.BlockSpec((tm, tc), lambda c, b, r: (b * steps + r, c)),
        out_shape=jax.ShapeDtypeStruct((bsz * t, C_CONV_CH), F32),
        scratch_shapes=[pltpu.VMEM((CONV_HALO, tc), F32)],
        compiler_params=_params("parallel", "parallel", "arbitrary"),
        name="conv_silu",
    )(proj_m, st8, conv_w.astype(F32))


def _odd_mixer(proj_m, proj_s, row0, bsz, t, past, conv_w, a_log, dt_bias, onorm):
    m = bsz * t
    ps = proj_s[row0:row0 + m]
    b_raw, a_raw = ps[:, :C_V_HEADS], ps[:, C_V_HEADS:2 * C_V_HEADS]
    if past is None:
        conv_state = jnp.zeros((bsz, C_CONV - 1, C_CONV_CH), F32)
        s0_c = jnp.zeros((bsz, C_V_HEADS, C_DK, C_DV), F32)
    else:
        s0_c, conv_state = past
    act = conv_silu(proj_m, row0, bsz, t, conv_state, conv_w)
    tail = [proj_m[row0 + b * t + t - (C_CONV - 1):row0 + (b + 1) * t, :C_CONV_CH] for b in range(bsz)]
    new_conv = jnp.stack(tail)
    beta = jax.nn.sigmoid(b_raw)
    g = -jnp.exp(a_log.astype(F32)) * jax.nn.softplus(a_raw + dt_bias.astype(F32))
    o_c, s_c = gdn_pallas(act, proj_m, beta, g, row0, bsz, t, onorm, s0_c)
    return o_c, (s_c, new_conv)


def _pad_cols(w, n):
    return jnp.pad(w, ((0, 0), (0, n - w.shape[1])))


def kernel(x_prompt, x_sample, cache_a_k, cache_a_v, cache_a_kidx, state_b, state_c, state_c_conv, p_prompt, p_sample, ffn1_norm, ffn1_w_gate, ffn1_w_up, ffn1_w_down, mix_norm, even_w_in, even_w_out, a_q_norm, a_k_norm, b_lb_logits, b_out_norm, odd_w_in, odd_w_out, c_conv_w, c_a_log, c_dt_bias, c_out_norm, ffn2_norm, ffn2_w_gate, ffn2_w_up, ffn2_w_down, ple_norm, ple_w_gate, ple_w_proj, ple_post_norm):
    d = D_MODEL
    bp, tp = x_prompt.shape[:2]
    bs, ts = x_sample.shape[:2]
    mp, ms = bp * tp, bs * ts
    x = jnp.concatenate([x_prompt.reshape(mp, d), x_sample.reshape(ms, d)], axis=0)
    p_all = jnp.concatenate([p_prompt.reshape(DEPTH, mp, -1), p_sample.reshape(DEPTH, ms, -1)],
                            axis=1).astype(BF16)

    past_len = cache_a_k.shape[2]
    pos_all = jnp.concatenate([jnp.tile(jnp.arange(tp, dtype=jnp.int32), bp),
                               jnp.tile(past_len + jnp.arange(ts, dtype=jnp.int32), bs)])

    lb_soft = jax.nn.softmax(b_lb_logits.astype(F32), axis=0)
    lbs = jnp.cumsum(lb_soft, axis=0) - lb_soft[0]

    new_p = [[] for _ in range(6)]
    new_s = [[] for _ in range(6)]
    for layer in range(DEPTH):
        j = layer // 2
        x = ffn(x, ffn1_norm[layer], ffn1_w_gate[layer].astype(BF16),
                ffn1_w_up[layer].astype(BF16), ffn1_w_down[layer].astype(BF16))
        h = rmsnorm(x, mix_norm[layer])
        if layer % 2 == 0:
            w_in = even_w_in[j]
            w_a = _pad_cols(w_in[:, :7328], 7680).astype(BF16)
            w_b = w_in[:, 7328:].astype(BF16)
            proj_a = matmul(h, w_a, name="even_in_a")
            proj_b = matmul(h, w_b, tn=1024, name="even_in_b")
            prep = even_prep(proj_a, pos_all, a_q_norm[j], a_k_norm[j])
            mix_p, st_p = _even_mixer(prep, proj_b, 0, bp, tp, None, lbs[j], b_out_norm[j])
            mix_s, st_s = _even_mixer(prep, proj_b, mp, bs, ts,
                                      (cache_a_k[j], cache_a_v[j], cache_a_kidx[j], state_b[j]),
                                      lbs[j], b_out_norm[j])
            for idx in range(4):
                new_p[idx].append(st_p[idx])
                new_s[idx].append(st_s[idx])
            w_out = even_w_out[j].astype(BF16)
        else:
            w_in = odd_w_in[j]
            w_m = w_in[:, :12288].astype(BF16)
            w_s = _pad_cols(w_in[:, 12288:], 128).astype(BF16)
            proj_m = matmul(h, w_m, tn=1024, name="odd_in_m")
            proj_s = matmul(h, w_s, tn=128, name="odd_in_s")
            mix_p, st_p = _odd_mixer(proj_m, proj_s, 0, bp, tp, None,
                                     c_conv_w[j], c_a_log[j], c_dt_bias[j], c_out_norm[j])
            mix_s, st_s = _odd_mixer(proj_m, proj_s, mp, bs, ts,
                                     (state_c[j], state_c_conv[j]),
                                     c_conv_w[j], c_a_log[j], c_dt_bias[j], c_out_norm[j])
            for idx in range(2):
                new_p[4 + idx].append(st_p[idx])
                new_s[4 + idx].append(st_s[idx])
            w_out = odd_w_out[j].astype(BF16)
        mix_in = jnp.concatenate([mix_p, mix_s], axis=0).astype(BF16)
        x = matmul(mix_in, w_out, x, body=_mm_resid_body, name="mix_out")
        x = ffn(x, ffn2_norm[layer], ffn2_w_gate[layer].astype(BF16),
                ffn2_w_up[layer].astype(BF16), ffn2_w_down[layer].astype(BF16))
        hp = rmsnorm(x, ple_norm[layer])
        pp = ple_proj(p_all[layer], ple_w_proj[layer].astype(BF16), ple_post_norm[layer])
        x = matmul(hp, ple_w_gate[layer].astype(BF16), x, pp, body=_mm_gate_body, name="ple_gate")

    y_prompt = x[:mp].reshape(bp, tp, d)
    y_sample = x[mp:].reshape(bs, ts, d)
    outs_p = tuple(jnp.stack(v) for v in new_p)
    outs_s = tuple(jnp.stack(v) for v in new_s)
    return (y_prompt, y_sample) + outs_p + outs_s
```

```python
from functools import partial

import numpy as np

import jax
import jax.numpy as jnp
from jax import lax
from jax.experimental import pallas as pl
from jax.experimental.pallas import tpu as pltpu

D_MODEL = 4096
DEPTH = 4
CHUNK = 64
QBLOCK = 128
EPS = 1e-6
NEG_BIG = -1e30
ROPE_THETA = 10000.0
HEAD_DIM = 128
A_HEADS = 16
A_KV_HEADS = 4
IDX_HEADS = 32
IDX_DIM = 128
IDX_ROPE_DIM = 64
TOPK_MAX = 256
IDX_SCALE = (IDX_HEADS * IDX_DIM) ** -0.5
B_HEADS = 16
B_DK = 128
B_DV = 128
C_QK_HEADS = 16
C_V_HEADS = 32
C_DK = 128
C_DV = 128
C_CONV = 4
C_CONV_CH = 2 * C_QK_HEADS * C_DK + C_V_HEADS * C_DV

V7X_VMEM_LIMIT_BYTES = 56 * 1024 * 1024

BF16 = jnp.bfloat16
F32 = jnp.float32


def _params(*sem):
    return pltpu.CompilerParams(dimension_semantics=sem,
                                vmem_limit_bytes=V7X_VMEM_LIMIT_BYTES)


def _rmsnorm_body(x_ref, w_ref, o_ref):
    x = x_ref[...]
    ms = jnp.mean(x * x, axis=-1, keepdims=True)
    o_ref[...] = (x * lax.rsqrt(ms + EPS) * w_ref[...]).astype(o_ref.dtype)


def rmsnorm(x, w, tm=512, out_dtype=BF16):
    m, d = x.shape
    return pl.pallas_call(
        _rmsnorm_body,
        grid=(m // tm,),
        in_specs=[pl.BlockSpec((tm, d), lambda i: (i, 0)),
                  pl.BlockSpec((1, d), lambda i: (0, 0))],
        out_specs=pl.BlockSpec((tm, d), lambda i: (i, 0)),
        out_shape=jax.ShapeDtypeStruct((m, d), out_dtype),
        compiler_params=_params("parallel"),
        name="rmsnorm",
    )(x, w.reshape(1, d).astype(F32))


def _mm_body(a_ref, w_ref, o_ref):
    o_ref[...] = jnp.dot(a_ref[...], w_ref[...], preferred_element_type=F32)


def _mm_resid_body(a_ref, w_ref, r_ref, o_ref):
    o_ref[...] = r_ref[...] + jnp.dot(a_ref[...], w_ref[...], preferred_element_type=F32)


def _mm_gate_body(a_ref, w_ref, r_ref, p_ref, o_ref):
    acc = jnp.dot(a_ref[...], w_ref[...], preferred_element_type=F32)
    o_ref[...] = r_ref[...] + jax.nn.sigmoid(acc) * p_ref[...]


def matmul(a, w, *extras, tm=1024, tn=512, body=_mm_body, name="matmul"):
    m, k = a.shape
    n = w.shape[1]
    tile = pl.BlockSpec((tm, tn), lambda i, j: (i, j))
    return pl.pallas_call(
        body,
        grid=(m // tm, n // tn),
        in_specs=[pl.BlockSpec((tm, k), lambda i, j: (i, 0)),
                  pl.BlockSpec((k, tn), lambda i, j: (0, j))] + [tile] * len(extras),
        out_specs=tile,
        out_shape=jax.ShapeDtypeStruct((m, n), F32),
        compiler_params=_params("parallel", "arbitrary"),
        name=name,
    )(a, w, *extras)


def _ple_proj_body(p_ref, w_ref, nw_ref, o_ref):
    y = jnp.dot(p_ref[...], w_ref[...], preferred_element_type=F32)
    ms = jnp.mean(y * y, axis=-1, keepdims=True)
    o_ref[...] = y * lax.rsqrt(ms + EPS) * nw_ref[...]


def ple_proj(p, w, nw, tm=512):
    m, k = p.shape
    n = w.shape[1]
    return pl.pallas_call(
        _ple_proj_body,
        grid=(m // tm,),
        in_specs=[pl.BlockSpec((tm, k), lambda i: (i, 0)),
                  pl.BlockSpec((k, n), lambda i: (0, 0)),
                  pl.BlockSpec((1, n), lambda i: (0, 0))],
        out_specs=pl.BlockSpec((tm, n), lambda i: (i, 0)),
        out_shape=jax.ShapeDtypeStruct((m, n), F32),
        compiler_params=_params("parallel"),
        name="ple_proj",
    )(p, w, nw.reshape(1, n).astype(F32))


FFN_NORM_ROWS = 64


def _ffn_body(x_ref, nw_ref, wg_ref, wu_ref, wd_ref, o_ref, h_ref):
    f = pl.program_id(1)

    @pl.when(f == 0)
    def _():
        for r0 in range(0, x_ref.shape[0], FFN_NORM_ROWS):
            rows = slice(r0, r0 + FFN_NORM_ROWS)
            x = x_ref[rows, :]
            ms = jnp.mean(x * x, axis=-1, keepdims=True)
            h_ref[rows, :] = (x * lax.rsqrt(ms + EPS) * nw_ref[...]).astype(BF16)
        o_ref[...] = jnp.zeros_like(o_ref)

    h = h_ref[...]
    g = jnp.dot(h, wg_ref[...], preferred_element_type=F32)
    u = jnp.dot(h, wu_ref[...], preferred_element_type=F32)
    act = (g * jax.nn.sigmoid(g) * u).astype(BF16)
    o_ref[...] += jnp.dot(act, wd_ref[...], preferred_element_type=F32)

    @pl.when(f == pl.num_programs(1) - 1)
    def _():
        o_ref[...] = x_ref[...] + 0.5 * o_ref[...]


def ffn(x, nw, wg, wu, wd, tm=768, tf=256):
    m, d = x.shape
    dff = wg.shape[1]
    return pl.pallas_call(
        _ffn_body,
        grid=(m // tm, dff // tf),
        in_specs=[pl.BlockSpec((tm, d), lambda i, f: (i, 0), pipeline_mode=pl.Buffered(1)),
                  pl.BlockSpec((1, d), lambda i, f: (0, 0)),
                  pl.BlockSpec((d, tf), lambda i, f: (0, f)),
                  pl.BlockSpec((d, tf), lambda i, f: (0, f)),
                  pl.BlockSpec((tf, d), lambda i, f: (f, 0))],
        out_specs=pl.BlockSpec((tm, d), lambda i, f: (i, 0)),
        out_shape=jax.ShapeDtypeStruct((m, d), F32),
        scratch_shapes=[pltpu.VMEM((tm, d), BF16)],
        compiler_params=_params("parallel", "arbitrary"),
        name="ffn",
    )(x, nw.reshape(1, d).astype(F32), wg, wu, wd)


DSA_KEY_TILE = 512
DSA_HEAD_GROUP = 8
DSA_HEADS_INTERLEAVED = 4
INT32_MIN = -2 ** 31
A_GROUP = A_HEADS // A_KV_HEADS
NT_DIMS = (((1,), (1,)), ((), ()))


def _dsa_body(qi_ref, wi_ref, qt_ref, ki_ref, k_ref, vt_ref, o_ref,
              qi_st, wib, skey, m_s, l_s, acc_s, *, qb, past_len, l_valid, topk, idx_bits):
    lk = DSA_KEY_TILE
    i = pl.program_id(1)
    q_pos0 = past_len + i * qb
    limit = jnp.minimum(((q_pos0 + qb - 1) // CHUNK + 1) * CHUNK, l_valid)
    nt = (limit + lk - 1) // lk

    for h in range(IDX_HEADS):
        qi_st[h * qb:(h + 1) * qb, :] = qi_ref[0, :, h * IDX_DIM:(h + 1) * IDX_DIM]
        wib[h] = jnp.broadcast_to(wi_ref[0, :, h:h + 1], (qb, 128))

    q_chunk = (q_pos0 + lax.broadcasted_iota(jnp.int32, (qb, 128), 0)) // CHUNK

    def score_tile(t, carry):
        kt = ki_ref[0, t]
        cols = [jnp.zeros((qb, 128), F32) for _ in range(lk // 128)]
        for hg in range(IDX_HEADS // DSA_HEAD_GROUP):
            rows = DSA_HEAD_GROUP * qb
            lg = lax.dot_general(qi_st[hg * rows:(hg + 1) * rows, :], kt, NT_DIMS,
                                 preferred_element_type=F32)
            for hh in range(DSA_HEAD_GROUP):
                w = wib[hg * DSA_HEAD_GROUP + hh]
                for c in range(lk // 128):
                    blk = lg[hh * qb:(hh + 1) * qb, c * 128:(c + 1) * 128]
                    cols[c] = cols[c] + jnp.maximum(blk, 0.0) * w
        for c in range(lk // 128):
            k_pos = t * lk + c * 128 + lax.broadcasted_iota(jnp.int32, (qb, 128), 1)
            adm = (k_pos // CHUNK <= q_chunk) & (k_pos < l_valid)
            bits = lax.bitcast_convert_type(cols[c], jnp.int32)
            key = jnp.where(bits < 0, bits ^ jnp.int32(0x7FFFFFFF), bits)
            skey[t, :, c * 128:(c + 1) * 128] = jnp.where(adm, key, jnp.int32(INT32_MIN))
        return carry

    lax.fori_loop(0, nt, score_tile, 0)

    def count_ge(cand):
        def body(t, acc):
            for c in range(lk // 128):
                acc = acc + jnp.where(skey[t, :, c * 128:(c + 1) * 128] >= cand, 1.0, 0.0)
            return acc
        acc = lax.fori_loop(0, nt, body, jnp.zeros((qb, 128), F32))
        return jnp.sum(acc, axis=1, keepdims=True)

    kf = jnp.float32(topk)
    zero = jnp.zeros((qb, 128), jnp.int32)
    prefix = jnp.where(count_ge(zero) >= kf, zero, jnp.int32(INT32_MIN))

    def bit_step(s, prefix):
        cand = prefix + lax.shift_left(jnp.int32(1), jnp.int32(30) - s)
        return jnp.where(count_ge(cand) >= kf, cand, prefix)

    prefix = lax.fori_loop(0, 31, bit_step, prefix)
    thr = jnp.maximum(prefix, jnp.int32(INT32_MIN + 1))

    real = prefix > jnp.int32(INT32_MIN)
    tie_rows = jnp.where((count_ge(prefix) > kf) & real[:, 0:1], 1.0, 0.0)

    @pl.when(jnp.max(tie_rows) > 0.0)
    def _():
        lane = lax.broadcasted_iota(jnp.int32, (qb, 128), 1)

        def count_tiles(pred):
            def body(t, acc):
                for c in range(lk // 128):
                    tile = skey[t, :, c * 128:(c + 1) * 128]
                    acc = acc + jnp.where(pred(tile, t * lk + c * 128 + lane), 1.0, 0.0)
                return acc
            acc = lax.fori_loop(0, nt, body, jnp.zeros((qb, 128), F32))
            return jnp.sum(acc, axis=1, keepdims=True)

        need = kf - count_tiles(lambda tile, kidx: tile > prefix)

        def cut_step(s, cut):
            cand = cut + lax.shift_left(jnp.int32(1), jnp.int32(idx_bits - 1) - s)
            before = count_tiles(lambda tile, kidx: (tile == prefix) & (kidx < cand))
            return jnp.where(before < need, cand, cut)

        cut = lax.fori_loop(0, idx_bits, cut_step, jnp.zeros((qb, 128), jnp.int32))

        def demote(t, carry):
            for c in range(lk // 128):
                cols = slice(c * 128, (c + 1) * 128)
                tile = skey[t, :, cols]
                drop = real & (tile == prefix) & (t * lk + c * 128 + lane > cut)
                skey[t, :, cols] = jnp.where(drop, prefix - 1, tile)
            return carry

        lax.fori_loop(0, nt, demote, 0)

    m_s[...] = jnp.full(m_s.shape, NEG_BIG, F32)
    l_s[...] = jnp.zeros(l_s.shape, F32)
    acc_s[...] = jnp.zeros(acc_s.shape, F32)

    eye_q = jnp.where(lax.broadcasted_iota(jnp.int32, (qb, qb), 0)
                      == lax.broadcasted_iota(jnp.int32, (qb, qb), 1), 1.0, 0.0).astype(BF16)
    eye_d = jnp.where(lax.broadcasted_iota(jnp.int32, (HEAD_DIM, HEAD_DIM), 0)
                      == lax.broadcasted_iota(jnp.int32, (HEAD_DIM, HEAD_DIM), 1), 1.0, 0.0).astype(BF16)

    def attn_tile(t, carry):
        sel = jnp.concatenate(
            [jnp.where(skey[t, :, c * 128:(c + 1) * 128] >= thr, 1.0, 0.0)
             for c in range(lk // 128)], axis=1).astype(BF16)
        sel_t = lax.dot_general(sel, eye_q, TN_DIMS, preferred_element_type=F32)
        bias = (sel_t - 1.0) * (-NEG_BIG)
        bias4 = jnp.concatenate([bias] * A_GROUP, axis=1)
        for n0 in range(0, A_KV_HEADS, DSA_HEADS_INTERLEAVED):
            ns = range(n0, n0 + DSA_HEADS_INTERLEAVED)
            s = [jnp.dot(k_ref[0, t, :, n * HEAD_DIM:(n + 1) * HEAD_DIM], qt_ref[0, 0, n],
                         preferred_element_type=F32) + bias4 for n in ns]
            m_old = [m_s[n] for n in ns]
            m_new = [jnp.maximum(mo, jnp.max(x, axis=0, keepdims=True)) for mo, x in zip(m_old, s)]
            alpha = [jnp.exp(mo - mn) for mo, mn in zip(m_old, m_new)]
            p = [jnp.exp(x - mn) for x, mn in zip(s, m_new)]
            pv = [jnp.dot(vt_ref[0, t, n], x.astype(BF16), preferred_element_type=F32)
                  for n, x in zip(ns, p)]
            for j, n in enumerate(ns):
                l_s[n] = alpha[j] * l_s[n] + jnp.sum(p[j], axis=0, keepdims=True)
                acc_s[n] = alpha[j] * acc_s[n] + pv[j]
                m_s[n] = m_new[j]
        return carry

    lax.fori_loop(0, nt, attn_tile, 0)

    for n in range(A_KV_HEADS):
        o = (acc_s[n] / l_s[n]).astype(BF16)
        for g in range(A_GROUP):
            hd = n * A_GROUP + g
            o_t = lax.dot_general(o[:, g * qb:(g + 1) * qb], eye_d, TN_DIMS,
                                  preferred_element_type=F32)
            o_ref[0, :, hd * HEAD_DIM:(hd + 1) * HEAD_DIM] = o_t.astype(o_ref.dtype)


def dsa_attention_pallas(q, k, v, qi, ki, wi, past_len):
    bsz, t, _ = q.shape
    l = k.shape[1]
    lk = DSA_KEY_TILE
    topk = min(TOPK_MAX, l // 4)
    qb = min(QBLOCK, t)
    n_tiles = -(-l // lk)
    pad = n_tiles * lk - l

    def tiles(a):
        a = jnp.pad(a, ((0, 0), (0, pad), (0, 0))).astype(BF16)
        return a.reshape(bsz, n_tiles, lk, a.shape[-1])

    nb = t // qb
    qs = q.reshape(bsz, nb, qb, A_KV_HEADS, A_GROUP, HEAD_DIM)
    qt = jnp.transpose(qs, (0, 1, 3, 5, 4, 2)).reshape(bsz, nb, A_KV_HEADS, HEAD_DIM, A_GROUP * qb)
    vt = jnp.transpose(tiles(v).reshape(bsz, n_tiles, lk, A_KV_HEADS, HEAD_DIM), (0, 1, 3, 4, 2))
    whole = lambda shape: pl.BlockSpec(shape, lambda b, i: (b,) + (0,) * (len(shape) - 1),
                                       pipeline_mode=pl.Buffered(1))
    rows = lambda width: pl.BlockSpec((1, qb, width), lambda b, i: (b, i, 0))
    body = partial(_dsa_body, qb=qb, past_len=past_len, l_valid=l, topk=topk,
                   idx_bits=(n_tiles * lk - 1).bit_length())
    return pl.pallas_call(
        body,
        grid=(bsz, nb),
        in_specs=[rows(IDX_HEADS * IDX_DIM), rows(IDX_HEADS),
                  pl.BlockSpec((1, 1, A_KV_HEADS, HEAD_DIM, A_GROUP * qb), lambda b, i: (b, i, 0, 0, 0)),
                  whole((1, n_tiles, lk, IDX_DIM)),
                  whole((1, n_tiles, lk, A_KV_HEADS * HEAD_DIM)),
                  whole((1, n_tiles, A_KV_HEADS, HEAD_DIM, lk))],
        out_specs=rows(A_HEADS * HEAD_DIM),
        out_shape=jax.ShapeDtypeStruct((bsz, t, A_HEADS * HEAD_DIM), BF16),
        scratch_shapes=[pltpu.VMEM((IDX_HEADS * qb, IDX_DIM), BF16),
                        pltpu.VMEM((IDX_HEADS, qb, 128), F32),
                        pltpu.VMEM((n_tiles, qb, lk), jnp.int32),
                        pltpu.VMEM((A_KV_HEADS, 1, A_GROUP * qb), F32),
                        pltpu.VMEM((A_KV_HEADS, 1, A_GROUP * qb), F32),
                        pltpu.VMEM((A_KV_HEADS, HEAD_DIM, A_GROUP * qb), F32)],
        compiler_params=_params("parallel", "arbitrary"),
        name="dsa_attention",
    )(qi, wi, qt, tiles(ki), tiles(k), vt)


GLA_LEVELS = 6
GLA_ROWS_PER_STEP = 512
GLA_PROBLEMS_PER_STEP = 16


def _gla_exponent_matrix():
    c = CHUNK
    a = np.zeros((GLA_LEVELS + 1, c, c), np.float32)
    a[0] = np.tril(np.ones((c, c), np.float32))
    for l in range(GLA_LEVELS):
        b = (c // 2) >> l
        for t in range(c):
            ref = (t // (2 * b)) * 2 * b + b
            if t >= ref:
                a[l + 1, t, ref + 1:t + 1] = 1.0
            else:
                a[l + 1, t, t + 1:ref + 1] = 1.0
    return a.reshape((GLA_LEVELS + 1) * c, c)


def _split3_dot(a_bf16, x):
    hi = x.astype(BF16)
    r1 = x - hi.astype(F32)
    mid = r1.astype(BF16)
    lo = (r1 - mid.astype(F32)).astype(BF16)
    return (jnp.dot(a_bf16, hi, preferred_element_type=F32)
            + jnp.dot(a_bf16, mid, preferred_element_type=F32)
            + jnp.dot(a_bf16, lo, preferred_element_type=F32))


TN_DIMS = (((0,), (0,)), ((), ()))


def _gla_body(q_ref, f_ref, i_ref, g_ref, lb_ref, nw_ref, a_ref, s0_ref, o_ref, sfin_ref, st_ref,
              *, n_chunks, heads):
    r = pl.program_id(2)
    c = CHUNK

    @pl.when(r == 0)
    def _():
        for j in range(heads):
            st_ref[j] = s0_ref[0, j].T

    nw = nw_ref[...]
    amat = a_ref[...]
    t_row = lax.broadcasted_iota(jnp.int32, (c, B_DK), 0)
    t_idx = lax.broadcasted_iota(jnp.int32, (c, c), 0)
    s_idx = lax.broadcasted_iota(jnp.int32, (c, c), 1)
    upper = [((t_row >> (GLA_LEVELS - 1 - l)) & 1) == 1 for l in range(GLA_LEVELS)]
    pair = [((t_idx >> (GLA_LEVELS - l)) == (s_idx >> (GLA_LEVELS - l)))
            & (((t_idx >> (GLA_LEVELS - 1 - l)) & 1) == 1)
            & (((s_idx >> (GLA_LEVELS - 1 - l)) & 1) == 0) for l in range(GLA_LEVELS)]
    eye = t_idx == s_idx

    probs = [(j, ci) for j in range(heads) for ci in range(n_chunks)]
    rows = [slice(ci * c, (ci + 1) * c) for _, ci in probs]
    cols = [slice(j * B_DK, (j + 1) * B_DK) for j, _ in probs]
    k = [(1.0 - lb_ref[:, cl]) * jax.nn.sigmoid(-f_ref[rw, cl]) for rw, cl in zip(rows, cols)]
    logf = [jnp.log1p(-jnp.minimum(x, 1.0 - 1e-6)) for x in k]
    qr = [q_ref[rw, cl] for rw, cl in zip(rows, cols)]
    q = [x * jax.nn.sigmoid(x) for x in qr]
    d_all = [_split3_dot(amat, x) for x in logf]
    att = [jnp.where(eye, jnp.sum(a * b, axis=1, keepdims=True), 0.0) for a, b in zip(q, k)]
    for l in range(GLA_LEVELS):
        x = [(jnp.where(upper[l], a, b) * jnp.exp(d[(l + 1) * c:(l + 2) * c])).astype(BF16)
             for a, b, d in zip(q, k, d_all)]
        gram = [lax.dot_general(y, y, NT_DIMS, preferred_element_type=F32) for y in x]
        att = [a + jnp.where(pair[l], g, 0.0) for a, g in zip(att, gram)]
    gcum = [d[0:c] for d in d_all]
    g_last = [x[c - 1:c] for x in gcum]
    q_dec = [(a * jnp.exp(gc)).astype(BF16) for a, gc in zip(q, gcum)]
    k_dec = [(a * jnp.exp(gl - gc)).astype(BF16) for a, gl, gc in zip(k, g_last, gcum)]
    vb = [i_ref[rw, cl].astype(BF16) for rw, cl in zip(rows, cols)]
    o_intra = [jnp.dot(a.astype(BF16), v, preferred_element_type=F32) for a, v in zip(att, vb)]
    kv = [lax.dot_general(v, kd, TN_DIMS, preferred_element_type=F32) for v, kd in zip(vb, k_dec)]
    e_last = [jnp.exp(x) for x in g_last]

    st = [st_ref[j] for j in range(heads)]
    for ci in range(n_chunks):
        ps = [j * n_chunks + ci for j in range(heads)]
        o = [o_intra[p] + lax.dot_general(q_dec[p], x.astype(BF16), NT_DIMS, preferred_element_type=F32)
             for p, x in zip(ps, st)]
        st = [x * e_last[p] + kv[p] for p, x in zip(ps, st)]
        for p, x in zip(ps, o):
            x = x * lax.rsqrt(jnp.mean(x * x, axis=1, keepdims=True) + EPS) * nw
            gr = g_ref[rows[p], cols[p]]
            o_ref[rows[p], cols[p]] = (x * (gr * jax.nn.sigmoid(gr))).astype(o_ref.dtype)
    for j in range(heads):
        st_ref[j] = st[j]

    @pl.when(r == pl.num_programs(2) - 1)
    def _():
        for j in range(heads):
            sfin_ref[0, j] = st[j].T


def hgrn2_pallas(proj_b, row0, bsz, t, lb, norm_w, s0):
    rps = min(t, GLA_ROWS_PER_STEP)
    n_chunks = rps // CHUNK
    heads = min(B_HEADS, GLA_PROBLEMS_PER_STEP // n_chunks)
    steps = t // rps
    blk0 = row0 // rps
    groups = B_HEADS // heads
    width = heads * B_DK
    seg = lambda s: pl.BlockSpec((rps, width), lambda b, h, r: (blk0 + b * steps + r, s * groups + h))
    state = pl.BlockSpec((1, heads, B_DK, B_DV), lambda b, h, r: (b, h, 0, 0))
    amat = jnp.asarray(_gla_exponent_matrix(), BF16)
    return pl.pallas_call(
        partial(_gla_body, n_chunks=n_chunks, heads=heads),
        grid=(bsz, groups, steps),
        in_specs=[seg(0), seg(1), seg(2), seg(3),
                  pl.BlockSpec((1, width), lambda b, h, r: (0, h)),
                  pl.BlockSpec((1, B_DV), lambda b, h, r: (0, 0)),
                  pl.BlockSpec(amat.shape, lambda b, h, r: (0, 0)),
                  state],
        out_specs=[pl.BlockSpec((rps, width), lambda b, h, r: (b * steps + r, h)), state],
        out_shape=[jax.ShapeDtypeStruct((bsz * t, B_HEADS * B_DV), BF16),
                   jax.ShapeDtypeStruct((bsz, B_HEADS, B_DK, B_DV), F32)],
        scratch_shapes=[pltpu.VMEM((heads, B_DV, B_DK), F32)],
        compiler_params=_params("parallel", "parallel", "arbitrary"),
        name="hgrn2",
    )(proj_b, proj_b, proj_b, proj_b, lb.reshape(1, -1).astype(F32),
      norm_w.reshape(1, -1).astype(F32), amat, s0)


GDN_PROBLEMS_PER_STEP = 32
GDN_CHUNKS_PER_STEP = 2


def _split3_dot_tn(x, u_bf16):
    hi = x.astype(BF16)
    r1 = x - hi.astype(F32)
    mid = r1.astype(BF16)
    lo = (r1 - mid.astype(F32)).astype(BF16)
    return (lax.dot_general(hi, u_bf16, TN_DIMS, preferred_element_type=F32)
            + lax.dot_general(mid, u_bf16, TN_DIMS, preferred_element_type=F32)
            + lax.dot_general(lo, u_bf16, TN_DIMS, preferred_element_type=F32))


def _gdn_body(q_ref, k_ref, v_ref, z_ref, beta_ref, g_ref, nw_ref, s0_ref, o_ref, sfin_ref, s_ref,
              *, n_chunks, heads):
    hg = pl.program_id(1)
    r = pl.program_id(2)
    c = CHUNK
    rep = C_V_HEADS // C_QK_HEADS

    @pl.when(r == 0)
    def _():
        s_ref[...] = s0_ref[0]

    nw = nw_ref[...]
    t_idx = lax.broadcasted_iota(jnp.int32, (c, c), 0)
    s_idx = lax.broadcasted_iota(jnp.int32, (c, c), 1)
    incl = s_idx <= t_idx
    strict = s_idx < t_idx
    eye = jnp.where(t_idx == s_idx, 1.0, 0.0)
    tril_b = jnp.where(incl, 1.0, 0.0).astype(BF16)
    triu_b = jnp.where(t_idx <= s_idx, 1.0, 0.0).astype(BF16)
    pair = [((t_idx >> (l + 1)) == (s_idx >> (l + 1)))
            & (((t_idx >> l) & 1) == 1) & (((s_idx >> l) & 1) == 0) for l in range(GLA_LEVELS)]
    lane_head = lax.broadcasted_iota(jnp.int32, (c, C_V_HEADS), 1)

    probs = [(j, ci) for j in range(heads) for ci in range(n_chunks)]
    cs = range(len(probs))
    rows = [slice(ci * c, (ci + 1) * c) for _, ci in probs]
    vcol = [slice(j * C_DV, (j + 1) * C_DV) for j, _ in probs]
    qcol = [slice((j // rep) * C_DK, (j // rep + 1) * C_DK) for j, _ in probs]
    head_sel = [lane_head == hg * heads + j for j, _ in probs]
    qa = [q_ref[rw, qc] for rw, qc in zip(rows, qcol)]
    ka = [k_ref[rw, qc] for rw, qc in zip(rows, qcol)]
    q = [x * lax.rsqrt(jnp.sum(x * x, axis=1, keepdims=True) + EPS) * (C_DK ** -0.5) for x in qa]
    k = [x * lax.rsqrt(jnp.sum(x * x, axis=1, keepdims=True) + EPS) for x in ka]
    beta = [jnp.sum(jnp.where(hs, beta_ref[rw, :], 0.0), axis=1, keepdims=True)
            for rw, hs in zip(rows, head_sel)]
    gb = [jnp.broadcast_to(jnp.sum(jnp.where(hs, g_ref[rw, :], 0.0), axis=1, keepdims=True),
                           (c, C_DK)) for rw, hs in zip(rows, head_sel)]
    gcum = [_split3_dot(tril_b, x) for x in gb]
    gc_row = [_split3_dot_tn(x[:, :c], triu_b) for x in gb]
    decay = [jnp.where(incl, jnp.exp(jnp.where(incl, a[:, :c] - b, 0.0)), 0.0)
             for a, b in zip(gcum, gc_row)]
    kb = [a * b for a, b in zip(k, beta)]
    kbf = [x.astype(BF16) for x in k]
    a_mat = [jnp.where(strict, lax.dot_general(a.astype(BF16), b, NT_DIMS,
                                               preferred_element_type=F32) * d, 0.0)
             for a, b, d in zip(kb, kbf, decay)]
    tinv = [eye for _ in cs]
    for l in range(GLA_LEVELS):
        tb = [x.astype(BF16) for x in tinv]
        xt = [jnp.dot(jnp.where(pair[l], a, 0.0).astype(BF16), b, preferred_element_type=F32)
              for a, b in zip(a_mat, tb)]
        tinv = [t0 - jnp.dot(b, x.astype(BF16), preferred_element_type=F32)
                for t0, b, x in zip(tinv, tb, xt)]
    e_cum = [jnp.exp(x) for x in gcum]
    rhs = [jnp.concatenate([v_ref[rw, vc] * bt, b * e], axis=1)
           for rw, vc, bt, b, e in zip(rows, vcol, beta, kb, e_cum)]
    sol = [x + jnp.dot((t0 - eye).astype(BF16), x.astype(BF16), preferred_element_type=F32)
           for x, t0 in zip(rhs, tinv)]
    qk = [jnp.where(incl, lax.dot_general(a.astype(BF16), b, NT_DIMS,
                                          preferred_element_type=F32) * d, 0.0).astype(BF16)
          for a, b, d in zip(q, kbf, decay)]
    g_last = [x[c - 1:c] for x in gcum]
    q_dec = [(a * e).astype(BF16) for a, e in zip(q, e_cum)]
    k_dec = [(a * jnp.exp(gl - gc)).astype(BF16) for a, gl, gc in zip(k, g_last, gcum)]

    s = [s_ref[j] for j in range(heads)]
    for ci in range(n_chunks):
        ps = [j * n_chunks + ci for j in range(heads)]
        sb = [x.astype(BF16) for x in s]
        ub = [(sol[p][:, :C_DV] - jnp.dot(sol[p][:, C_DV:].astype(BF16), b,
                                          preferred_element_type=F32)).astype(BF16)
              for p, b in zip(ps, sb)]
        o = [jnp.dot(q_dec[p], b, preferred_element_type=F32)
             + jnp.dot(qk[p], u, preferred_element_type=F32) for p, b, u in zip(ps, sb, ub)]
        s = [jnp.exp(g_last[p]) * x + lax.dot_general(k_dec[p], u, TN_DIMS, preferred_element_type=F32)
             for p, x, u in zip(ps, s, ub)]
        for p, x in zip(ps, o):
            x = x * lax.rsqrt(jnp.mean(x * x, axis=1, keepdims=True) + EPS) * nw
            z = z_ref[rows[p], vcol[p]]
            o_ref[rows[p], vcol[p]] = (x * (z * jax.nn.sigmoid(z))).astype(o_ref.dtype)
    for j in range(heads):
        s_ref[j] = s[j]

    @pl.when(r == pl.num_programs(2) - 1)
    def _():
        for j in range(heads):
            sfin_ref[0, j] = s[j]


def gdn_pallas(act, proj_m, beta, g, row0, bsz, t, norm_w, s0):
    n_chunks = min(t // CHUNK, GDN_CHUNKS_PER_STEP)
    heads = GDN_PROBLEMS_PER_STEP // n_chunks
    rps = n_chunks * CHUNK
    steps = t // rps
    blk0 = row0 // rps
    rep = C_V_HEADS // C_QK_HEADS
    qk_w = (heads // rep) * C_DK
    v_w = heads * C_DV
    qk_blocks = C_QK_HEADS * C_DK // qk_w
    qk_spec = lambda seg: pl.BlockSpec((rps, qk_w), lambda b, h, r: (b * steps + r, seg * qk_blocks + h))
    small = pl.BlockSpec((rps, C_V_HEADS), lambda b, h, r: (b * steps + r, 0))
    state = pl.BlockSpec((1, heads, C_DK, C_DV), lambda b, h, r: (b, h, 0, 0))
    return pl.pallas_call(
        partial(_gdn_body, n_chunks=n_chunks, heads=heads),
        grid=(bsz, C_V_HEADS // heads, steps),
        in_specs=[qk_spec(0), qk_spec(1),
                  pl.BlockSpec((rps, v_w), lambda b, h, r: (b * steps + r, 2 * C_QK_HEADS * C_DK // v_w + h)),
                  pl.BlockSpec((rps, v_w), lambda b, h, r: (blk0 + b * steps + r, C_CONV_CH // v_w + h)),
                  small, small,
                  pl.BlockSpec((1, C_DV), lambda b, h, r: (0, 0)),
                  state],
        out_specs=[pl.BlockSpec((rps, v_w), lambda b, h, r: (b * steps + r, h)), state],
        out_shape=[jax.ShapeDtypeStruct((bsz * t, C_V_HEADS * C_DV), BF16),
                   jax.ShapeDtypeStruct((bsz, C_V_HEADS, C_DK, C_DV), F32)],
        scratch_shapes=[pltpu.VMEM((heads, C_DK, C_DV), F32)],
        compiler_params=_params("parallel", "parallel", "arbitrary"),
        name="gated_deltanet",
    )(act, act, act, proj_m, beta, g, norm_w.reshape(1, -1).astype(F32), s0)


EVEN_A_COLS = 7680
OFF_AK, OFF_AV, OFF_IQ, OFF_IK, OFF_IW = 2048, 2560, 3072, 7168, 7296


def _rope_tables(pos):
    posf = pos.astype(F32)[:, None]

    def cs(half):
        inv_freq = ROPE_THETA ** (-jnp.arange(half, dtype=F32) / half)
        ang = posf * inv_freq[None, :]
        return jnp.cos(ang), jnp.sin(ang)

    c64, s64 = cs(HEAD_DIM // 2)
    c32, s32 = cs(IDX_ROPE_DIM // 2)
    z32 = jnp.zeros_like(s32)
    rest = IDX_DIM - IDX_ROPE_DIM
    ones = jnp.ones((pos.shape[0], rest), F32)
    zeros = jnp.zeros((pos.shape[0], rest), F32)
    return (jnp.concatenate([c64, c64], 1), jnp.concatenate([-s64, s64], 1),
            jnp.concatenate([c32, c32, ones], 1), jnp.concatenate([-s32, z32, zeros], 1),
            jnp.concatenate([z32, s32, zeros], 1))


def _even_prep_body(p_ref, ca_ref, sa_ref, ci_ref, s1_ref, s2_ref, qn_ref, kn_ref,
                    q_ref, k_ref, kb_ref, v_ref, vb_ref, iq_ref, ik_ref, ikb_ref, wi_ref):
    ca, sa = ca_ref[...], sa_ref[...]
    ci, s1, s2 = ci_ref[...], s1_ref[...], s2_ref[...]

    def head(off, j):
        return p_ref[:, off + j * HEAD_DIM:off + (j + 1) * HEAD_DIM]

    def norm_rope(x, w):
        y = x * lax.rsqrt(jnp.mean(x * x, axis=1, keepdims=True) + EPS) * w
        return y * ca + pltpu.roll(y, HEAD_DIM // 2, 1) * sa

    def idx_rope(x):
        half = IDX_ROPE_DIM // 2
        return x * ci + pltpu.roll(x, IDX_DIM - half, 1) * s1 + pltpu.roll(x, half, 1) * s2

    qn, kn = qn_ref[...], kn_ref[...]
    for j in range(A_HEADS):
        cols = slice(j * HEAD_DIM, (j + 1) * HEAD_DIM)
        q_ref[:, cols] = (norm_rope(head(0, j), qn) * (HEAD_DIM ** -0.5)).astype(BF16)
    for j in range(A_KV_HEADS):
        cols = slice(j * HEAD_DIM, (j + 1) * HEAD_DIM)
        kk = norm_rope(head(OFF_AK, j), kn)
        k_ref[:, cols] = kk
        kb_ref[:, cols] = kk.astype(BF16)
        vv = head(OFF_AV, j)
        v_ref[:, cols] = vv
        vb_ref[:, cols] = vv.astype(BF16)
    for j in range(IDX_HEADS):
        cols = slice(j * IDX_DIM, (j + 1) * IDX_DIM)
        iq_ref[:, cols] = idx_rope(head(OFF_IQ, j)).astype(BF16)
    ik = idx_rope(head(OFF_IK, 0))
    ik_ref[...] = ik
    ikb_ref[...] = ik.astype(BF16)
    wi_ref[...] = p_ref[:, OFF_IW:OFF_IW + IDX_HEADS] * IDX_SCALE


def even_prep(proj_a, pos, q_norm, k_norm, tm=256):
    m = proj_a.shape[0]
    tabs = _rope_tables(pos)
    row = lambda w: pl.BlockSpec((tm, w), lambda i: (i, 0))
    vec = pl.BlockSpec((1, HEAD_DIM), lambda i: (0, 0))
    kvw = A_KV_HEADS * HEAD_DIM
    shapes = [(A_HEADS * HEAD_DIM, BF16), (kvw, F32), (kvw, BF16), (kvw, F32), (kvw, BF16),
              (IDX_HEADS * IDX_DIM, BF16), (IDX_DIM, F32), (IDX_DIM, BF16), (IDX_HEADS, F32)]
    return pl.pallas_call(
        _even_prep_body,
        grid=(m // tm,),
        in_specs=[row(EVEN_A_COLS)] + [row(HEAD_DIM)] * 5 + [vec, vec],
        out_specs=[row(w) for w, _ in shapes],
        out_shape=[jax.ShapeDtypeStruct((m, w), dt) for w, dt in shapes],
        compiler_params=_params("parallel"),
        name="even_prep",
    )(proj_a, *tabs, q_norm.reshape(1, -1).astype(F32), k_norm.reshape(1, -1).astype(F32))


def _even_mixer(prep, proj_b, row0, bsz, t, past, lb, bnorm):
    m = bsz * t
    q_bf, k_f, k_bf, v_f, v_bf, iq_bf, ik_f, ik_bf, wi = [a[row0:row0 + m].reshape(bsz, t, -1) for a in prep]
    past_len = 0 if past is None else past[0].shape[1]
    if past is None:
        keys, vals, ikeys = k_bf, v_bf, ik_bf
        s0_b = jnp.zeros((bsz, B_HEADS, B_DK, B_DV), F32)
    else:
        flat = lambda a: a.reshape(bsz, past_len, -1).astype(BF16)
        keys = jnp.concatenate([flat(past[0]), k_bf], axis=1)
        vals = jnp.concatenate([flat(past[1]), v_bf], axis=1)
        ikeys = jnp.concatenate([flat(past[2]), ik_bf], axis=1)
        s0_b = past[3]
    o_a = dsa_attention_pallas(q_bf, keys, vals, iq_bf, ikeys, wi, past_len)
    o_b, s_b = hgrn2_pallas(proj_b, row0, bsz, t, lb, bnorm, s0_b)
    mixed_in = jnp.concatenate([o_a.reshape(m, -1), o_b], axis=-1)
    new_k = k_f.reshape(bsz, t, A_KV_HEADS, HEAD_DIM)
    new_v = v_f.reshape(bsz, t, A_KV_HEADS, HEAD_DIM)
    return mixed_in, (new_k, new_v, ik_f, s_b)


CONV_HALO = 8


def _conv_body(x_ref, st_ref, w_ref, o_ref, carry_ref):
    r = pl.program_id(2)
    x = x_ref[...]
    tm = x.shape[0]
    prev = jnp.where(r == 0, st_ref[0], carry_ref[...])
    top = x[0:CONV_HALO]
    row = lax.broadcasted_iota(jnp.int32, top.shape, 0)
    acc = x * w_ref[C_CONV - 1:C_CONV]
    acc_top = top * w_ref[C_CONV - 1:C_CONV]
    for k in range(1, C_CONV):
        w = w_ref[C_CONV - 1 - k:C_CONV - k]
        acc = acc + pltpu.roll(x, k, 0) * w
        acc_top = acc_top + jnp.where(row >= k, pltpu.roll(top, k, 0), pltpu.roll(prev, k, 0)) * w
    o_ref[...] = acc * jax.nn.sigmoid(acc)
    o_ref[0:CONV_HALO] = acc_top * jax.nn.sigmoid(acc_top)
    carry_ref[...] = x[tm - CONV_HALO:tm]


def conv_silu(proj_m, row0, bsz, t, conv_state, conv_w, tc=1024):
    tm = min(t, 512)
    steps = t // tm
    blk0 = row0 // tm
    st8 = jnp.pad(conv_state, ((0, 0), (CONV_HALO - (C_CONV - 1), 0), (0, 0)))
    return pl.pallas_call(
        _conv_body,
        grid=(C_CONV_CH // tc, bsz, steps),
        in_specs=[pl.BlockSpec((tm, tc), lambda c, b, r: (blk0 + b * steps + r, c)),
                  pl.BlockSpec((1, CONV_HALO, tc), lambda c, b, r: (b, 0, c)),
                  pl.BlockSpec((C_CONV, tc), lambda c, b, r: (0, c))],
        out_specs=pl.BlockSpec((tm, tc), lambda c, b, r: (b * steps + r, c)),
        out_shape=jax.ShapeDtypeStruct((bsz * t, C_CONV_CH), F32),
        scratch_shapes=[pltpu.VMEM((CONV_HALO, tc), F32)],
        compiler_params=_params("parallel", "parallel", "arbitrary"),
        name="conv_silu",
    )(proj_m, st8, conv_w.astype(F32))


def _odd_mixer(proj_m, proj_s, row0, bsz, t, past, conv_w, a_log, dt_bias, onorm):
    m = bsz * t
    ps = proj_s[row0:row0 + m]
    b_raw, a_raw = ps[:, :C_V_HEADS], ps[:, C_V_HEADS:2 * C_V_HEADS]
    if past is None:
        conv_state = jnp.zeros((bsz, C_CONV - 1, C_CONV_CH), F32)
        s0_c = jnp.zeros((bsz, C_V_HEADS, C_DK, C_DV), F32)
    else:
        s0_c, conv_state = past
    act = conv_silu(proj_m, row0, bsz, t, conv_state, conv_w)
    tail = [proj_m[row0 + b * t + t - (C_CONV - 1):row0 + (b + 1) * t, :C_CONV_CH] for b in range(bsz)]
    new_conv = jnp.stack(tail)
    beta = jax.nn.sigmoid(b_raw)
    g = -jnp.exp(a_log.astype(F32)) * jax.nn.softplus(a_raw + dt_bias.astype(F32))
    o_c, s_c = gdn_pallas(act, proj_m, beta, g, row0, bsz, t, onorm, s0_c)
    return o_c, (s_c, new_conv)


LANES = 128


def _realign_body(a_ref, b_ref, o_ref, *, off):
    lane = lax.broadcasted_iota(jnp.int32, a_ref.shape[1:], 1)
    lo = pltpu.roll(a_ref[0], LANES - off, 1)
    hi = pltpu.roll(b_ref[0], LANES - off, 1)
    o_ref[...] = jnp.where(lane < LANES - off, lo, hi).astype(o_ref.dtype)


def realign_cols(w, layer, start, width):
    k = w.shape[1]
    off = start % LANES
    blk0 = start // LANES
    src = lambda shift: pl.BlockSpec((1, k, LANES), lambda c: (layer, 0, blk0 + shift + c))
    return pl.pallas_call(
        partial(_realign_body, off=off),
        grid=(width // LANES,),
        in_specs=[src(0), src(1)],
        out_specs=pl.BlockSpec((k, LANES), lambda c: (0, c)),
        out_shape=jax.ShapeDtypeStruct((k, width), BF16),
        compiler_params=_params("parallel"),
        name="realign_cols",
    )(w, w)


def _pad_cols(w, n):
    return jnp.pad(w, ((0, 0), (0, n - w.shape[1])))


def kernel(x_prompt, x_sample, cache_a_k, cache_a_v, cache_a_kidx, state_b, state_c, state_c_conv, p_prompt, p_sample, ffn1_norm, ffn1_w_gate, ffn1_w_up, ffn1_w_down, mix_norm, even_w_in, even_w_out, a_q_norm, a_k_norm, b_lb_logits, b_out_norm, odd_w_in, odd_w_out, c_conv_w, c_a_log, c_dt_bias, c_out_norm, ffn2_norm, ffn2_w_gate, ffn2_w_up, ffn2_w_down, ple_norm, ple_w_gate, ple_w_proj, ple_post_norm):
    d = D_MODEL
    bp, tp = x_prompt.shape[:2]
    bs, ts = x_sample.shape[:2]
    mp, ms = bp * tp, bs * ts
    x = jnp.concatenate([x_prompt.reshape(mp, d), x_sample.reshape(ms, d)], axis=0)
    p_all = jnp.concatenate([p_prompt.reshape(DEPTH, mp, -1), p_sample.reshape(DEPTH, ms, -1)],
                            axis=1).astype(BF16)

    past_len = cache_a_k.shape[2]
    pos_all = jnp.concatenate([jnp.tile(jnp.arange(tp, dtype=jnp.int32), bp),
                               jnp.tile(past_len + jnp.arange(ts, dtype=jnp.int32), bs)])

    lb_soft = jax.nn.softmax(b_lb_logits.astype(F32), axis=0)
    lbs = jnp.cumsum(lb_soft, axis=0) - lb_soft[0]

    new_p = [[] for _ in range(6)]
    new_s = [[] for _ in range(6)]
    for layer in range(DEPTH):
        j = layer // 2
        x = ffn(x, ffn1_norm[layer], ffn1_w_gate[layer].astype(BF16),
                ffn1_w_up[layer].astype(BF16), ffn1_w_down[layer].astype(BF16))
        h = rmsnorm(x, mix_norm[layer])
        if layer % 2 == 0:
            w_in = even_w_in[j]
            w_a = _pad_cols(w_in[:, :7328], 7680).astype(BF16)
            w_b = realign_cols(even_w_in, j, 7328, 8192)
            proj_a = matmul(h, w_a, name="even_in_a")
            proj_b = matmul(h, w_b, tn=1024, name="even_in_b")
            prep = even_prep(proj_a, pos_all, a_q_norm[j], a_k_norm[j])
            mix_p, st_p = _even_mixer(prep, proj_b, 0, bp, tp, None, lbs[j], b_out_norm[j])
            mix_s, st_s = _even_mixer(prep, proj_b, mp, bs, ts,
                                      (cache_a_k[j], cache_a_v[j], cache_a_kidx[j], state_b[j]),
                                      lbs[j], b_out_norm[j])
            for idx in range(4):
                new_p[idx].append(st_p[idx])
                new_s[idx].append(st_s[idx])
            w_out = even_w_out[j].astype(BF16)
        else:
            w_in = odd_w_in[j]
            w_m = w_in[:, :12288].astype(BF16)
            w_s = _pad_cols(w_in[:, 12288:], 128).astype(BF16)
            proj_m = matmul(h, w_m, tn=1024, name="odd_in_m")
            proj_s = matmul(h, w_s, tn=128, name="odd_in_s")
            mix_p, st_p = _odd_mixer(proj_m, proj_s, 0, bp, tp, None,
                                     c_conv_w[j], c_a_log[j], c_dt_bias[j], c_out_norm[j])
            mix_s, st_s = _odd_mixer(proj_m, proj_s, mp, bs, ts,
                                     (state_c[j], state_c_conv[j]),
                                     c_conv_w[j], c_a_log[j], c_dt_bias[j], c_out_norm[j])
            for idx in range(2):
                new_p[4 + idx].append(st_p[idx])
                new_s[4 + idx].append(st_s[idx])
            w_out = odd_w_out[j].astype(BF16)
        mix_in = jnp.concatenate([mix_p, mix_s], axis=0).astype(BF16)
        x = matmul(mix_in, w_out, x, body=_mm_resid_body, name="mix_out")
        x = ffn(x, ffn2_norm[layer], ffn2_w_gate[layer].astype(BF16),
                ffn2_w_up[layer].astype(BF16), ffn2_w_down[layer].astype(BF16))
        hp = rmsnorm(x, ple_norm[layer])
        pp = ple_proj(p_all[layer], ple_w_proj[layer].astype(BF16), ple_post_norm[layer])
        x = matmul(hp, ple_w_gate[layer].astype(BF16), x, pp, body=_mm_gate_body, name="ple_gate")

    y_prompt = x[:mp].reshape(bp, tp, d)
    y_sample = x[mp:].reshape(bs, ts, d)
    outs_p = tuple(jnp.stack(v) for v in new_p)
    outs_s = tuple(jnp.stack(v) for v in new_s)
    return (y_prompt, y_sample) + outs_p + outs_s
```
